```python
import jax, jax.numpy as jnp
from jax import lax
import numpy as np

D_MODEL = 1024
BATCH = 4
SEQ = 4096
DEPTH = 2
DEC_BATCH = 128
DEC_SEQ = 4
PAST_LEN = 2048
PAGE_SIZE = 128

N_MIXERS = 2
N_CONV_LAYERS = (DEPTH + N_MIXERS - 1) // N_MIXERS
N_ATTN_LAYERS = DEPTH // N_MIXERS
CONV_WIDTH = 31
D_CONV = D_MODEL
N_HEADS = 16
HEAD_DIM = D_MODEL // N_HEADS
Q_BLOCK = 128
D_FF = 2816
FFN_CONV_WIDTH = 3
LOGIT_BIAS_INIT = -6.0
EPS = 1e-6

kernel_name = "hybrid_conformer_conv_stick_breaking_step"


def rms_norm(x, g):
    xf = x.astype(jnp.float32)
    y = xf * lax.rsqrt(jnp.mean(xf * xf, axis=-1, keepdims=True) + EPS)
    return (y * g.astype(jnp.float32)).astype(x.dtype)


def layer_norm(x, g, b):
    xf = x.astype(jnp.float32)
    mu = jnp.mean(xf, axis=-1, keepdims=True)
    var = jnp.mean(jnp.square(xf - mu), axis=-1, keepdims=True)
    y = (xf - mu) * lax.rsqrt(var + EPS)
    return (y * g.astype(jnp.float32) + b.astype(jnp.float32)).astype(x.dtype)


def causal_dwconv(x_ext, w, b):
    c = x_ext.shape[-1]
    y = lax.conv_general_dilated(x_ext, w[:, None, :].astype(x_ext.dtype), window_strides=(1,), padding='VALID',
                                 dimension_numbers=('NWC', 'WIO', 'NWC'), feature_group_count=c)
    return y + b.astype(y.dtype)


def conformer_conv(h, past, w_in, b_in, w_dw, b_dw, ln_g, ln_b, w_out, b_out):
    u = h @ w_in + b_in
    a, gate = jnp.split(u, 2, axis=-1)
    glu = a * jax.nn.sigmoid(gate)
    ext = jnp.concatenate([past, glu], axis=1)
    y = causal_dwconv(ext, w_dw, b_dw)
    y = jax.nn.silu(layer_norm(y, ln_g, ln_b))
    return y @ w_out + b_out, ext[:, -(CONV_WIDTH - 1):]


def conv_ffn(h, past, w_gate, w_up, w_dw, b_dw, w_down):
    g = h @ w_gate
    ext = jnp.concatenate([past, g], axis=1)
    g_conv = causal_dwconv(ext, w_dw, b_dw)
    out = (jax.nn.silu(g_conv) * (h @ w_up)) @ w_down
    return out, ext[:, -(FFN_CONV_WIDTH - 1):]


def stick_breaking_block(q, k, v, bias, q_start):
    tq, tk = q.shape[1], k.shape[1]
    z = jnp.einsum('bqhd,bkhd->bhqk', q.astype(jnp.float32), k.astype(jnp.float32)) * (HEAD_DIM ** -0.5)
    z = z + bias.astype(jnp.float32)[None, :, None, None]
    q_pos = q_start + jnp.arange(tq)
    k_pos = jnp.arange(tk)
    valid = k_pos[None, :] < q_pos[:, None]
    log_fail = jnp.where(valid, jax.nn.log_sigmoid(-z), 0.0)
    after = lax.cumsum(log_fail, axis=3, reverse=True) - log_fail
    weights = jnp.where(valid, jnp.exp(jax.nn.log_sigmoid(z) + after), 0.0)
    out = jnp.einsum('bhqk,bkhd->bqhd', weights, v.astype(jnp.float32))
    return out.astype(v.dtype)


def stick_breaking_attention(q, k, v, bias, q_start):
    tq = q.shape[1]
    outs = []
    for b0 in range(0, tq, Q_BLOCK):
        b1 = min(b0 + Q_BLOCK, tq)
        k_end = q_start + b1
        outs.append(stick_breaking_block(q[:, b0:b1], k[:, :k_end], v[:, :k_end], bias, q_start + b0))
    return jnp.concatenate(outs, axis=1)


def run_trunk(x, conv_past, ffn_past, k_past, v_past, past_len,
              norm_mix, norm_ffn, norm_final, conv_w_in, conv_b_in, conv_w_dw, conv_b_dw, conv_ln_g,
              conv_ln_b, conv_w_out, conv_b_out, attn_w_qkv, attn_w_o, attn_b_logit, ffn_w_gate, ffn_w_up,
              ffn_w_dw, ffn_b_dw, ffn_w_down):
    b, t, _ = x.shape
    new_conv, new_ffn, new_k, new_v = [], [], [], []
    for i in range(DEPTH):
        h = rms_norm(x, norm_mix[i])
        if i % N_MIXERS == 0:
            c = i // N_MIXERS
            out, st = conformer_conv(h, conv_past[c], conv_w_in[c], conv_b_in[c], conv_w_dw[c], conv_b_dw[c],
                                     conv_ln_g[c], conv_ln_b[c], conv_w_out[c], conv_b_out[c])
            new_conv.append(st)
        else:
            a = i // N_MIXERS
            qkv = (h @ attn_w_qkv[a]).reshape(b, t, 3, N_HEADS, HEAD_DIM)
            q, k, v = qkv[:, :, 0], qkv[:, :, 1], qkv[:, :, 2]
            k_all = jnp.concatenate([k_past[a], k], axis=1)
            v_all = jnp.concatenate([v_past[a], v], axis=1)
            att = stick_breaking_attention(q, k_all, v_all, attn_b_logit[a], past_len)
            out = att.reshape(b, t, N_HEADS * HEAD_DIM) @ attn_w_o[a]
            new_k.append(k)
            new_v.append(v)
        x = x + out
        h = rms_norm(x, norm_ffn[i])
        out, st = conv_ffn(h, ffn_past[i], ffn_w_gate[i], ffn_w_up[i], ffn_w_dw[i], ffn_b_dw[i], ffn_w_down[i])
        new_ffn.append(st)
        x = x + out
    y = rms_norm(x, norm_final)
    return y, jnp.stack(new_conv), jnp.stack(new_ffn), jnp.stack(new_k), jnp.stack(new_v)


def setup_inputs(seed: int = 0) -> dict:
    key = jax.random.key(seed)
    ks = jax.random.split(key, 32)
    n_pages = PAST_LEN // PAGE_SIZE
    n_used = DEC_BATCH * n_pages
    n_pool = (5 * n_used + 3) // 4
    f32 = jnp.float32

    def nrm(k, shape, scale):
        return jax.random.normal(k, shape, f32) * scale

    page_table = jax.random.permutation(ks[0], n_pool)[:n_used].reshape(DEC_BATCH, n_pages).astype(jnp.int32)
    return {
        "x_prompt": nrm(ks[1], (BATCH, SEQ, D_MODEL), 1.0),
        "x_sample": nrm(ks[2], (DEC_BATCH, DEC_SEQ, D_MODEL), 1.0),
        "state_conv": nrm(ks[3], (N_CONV_LAYERS, DEC_BATCH, CONV_WIDTH - 1, D_CONV), 1.0),
        "state_ffn": nrm(ks[4], (DEPTH, DEC_BATCH, FFN_CONV_WIDTH - 1, D_FF), 1.0),
        "cache_k": nrm(ks[5], (N_ATTN_LAYERS, n_pool, PAGE_SIZE, N_HEADS, HEAD_DIM), 1.0),
        "cache_v": nrm(ks[6], (N_ATTN_LAYERS, n_pool, PAGE_SIZE, N_HEADS, HEAD_DIM), 1.0),
        "page_table": page_table,
        "norm_mix": 1.0 + nrm(ks[7], (DEPTH, D_MODEL), 0.02),
        "norm_ffn": 1.0 + nrm(ks[8], (DEPTH, D_MODEL), 0.02),
        "norm_final": 1.0 + nrm(ks[9], (D_MODEL,), 0.02),
        "conv_w_in": nrm(ks[10], (N_CONV_LAYERS, D_MODEL, 2 * D_CONV), D_MODEL ** -0.5),
        "conv_b_in": nrm(ks[11], (N_CONV_LAYERS, 2 * D_CONV), 0.02),
        "conv_w_dw": nrm(ks[12], (N_CONV_LAYERS, CONV_WIDTH, D_CONV), CONV_WIDTH ** -0.5),
        "conv_b_dw": nrm(ks[13], (N_CONV_LAYERS, D_CONV), 0.02),
        "conv_ln_g": 1.0 + nrm(ks[14], (N_CONV_LAYERS, D_CONV), 0.02),
        "conv_ln_b": nrm(ks[15], (N_CONV_LAYERS, D_CONV), 0.02),
        "conv_w_out": nrm(ks[16], (N_CONV_LAYERS, D_CONV, D_MODEL), D_CONV ** -0.5),
        "conv_b_out": nrm(ks[17], (N_CONV_LAYERS, D_MODEL), 0.02),
        "attn_w_qkv": nrm(ks[18], (N_ATTN_LAYERS, D_MODEL, 3 * N_HEADS * HEAD_DIM), D_MODEL ** -0.5),
        "attn_w_o": nrm(ks[19], (N_ATTN_LAYERS, N_HEADS * HEAD_DIM, D_MODEL), (N_HEADS * HEAD_DIM) ** -0.5),
        "attn_b_logit": LOGIT_BIAS_INIT + nrm(ks[25], (N_ATTN_LAYERS, N_HEADS), 0.5),
        "ffn_w_gate": nrm(ks[20], (DEPTH, D_MODEL, D_FF), D_MODEL ** -0.5),
        "ffn_w_up": nrm(ks[21], (DEPTH, D_MODEL, D_FF), D_MODEL ** -0.5),
        "ffn_w_dw": nrm(ks[22], (DEPTH, FFN_CONV_WIDTH, D_FF), FFN_CONV_WIDTH ** -0.5),
        "ffn_b_dw": nrm(ks[23], (DEPTH, D_FF), 0.02),
        "ffn_w_down": nrm(ks[24], (DEPTH, D_FF, D_MODEL), D_FF ** -0.5),
    }


def reference(x_prompt, x_sample, state_conv, state_ffn, cache_k, cache_v, page_table,
              norm_mix, norm_ffn, norm_final, conv_w_in, conv_b_in, conv_w_dw, conv_b_dw, conv_ln_g,
              conv_ln_b, conv_w_out, conv_b_out, attn_w_qkv, attn_w_o, attn_b_logit, ffn_w_gate, ffn_w_up,
              ffn_w_dw, ffn_b_dw, ffn_w_down):
    weights = (norm_mix, norm_ffn, norm_final, conv_w_in, conv_b_in, conv_w_dw, conv_b_dw, conv_ln_g,
               conv_ln_b, conv_w_out, conv_b_out, attn_w_qkv, attn_w_o, attn_b_logit, ffn_w_gate, ffn_w_up,
               ffn_w_dw, ffn_b_dw, ffn_w_down)
    dt = x_prompt.dtype
    b = x_prompt.shape[0]
    conv0 = jnp.zeros((N_CONV_LAYERS, b, CONV_WIDTH - 1, D_CONV), dt)
    ffn0 = jnp.zeros((DEPTH, b, FFN_CONV_WIDTH - 1, D_FF), dt)
    kv0 = [jnp.zeros((b, 0, N_HEADS, HEAD_DIM), dt) for _ in range(N_ATTN_LAYERS)]
    y_p, sc_p, sf_p, k_p, v_p = run_trunk(x_prompt, conv0, ffn0, kv0, kv0, 0, *weights)

    db = x_sample.shape[0]
    past_len = page_table.shape[1] * PAGE_SIZE
    k_past = [jnp.take(cache_k[a], page_table, axis=0).reshape(db, past_len, N_HEADS, HEAD_DIM)
              for a in range(N_ATTN_LAYERS)]
    v_past = [jnp.take(cache_v[a], page_table, axis=0).reshape(db, past_len, N_HEADS, HEAD_DIM)
              for a in range(N_ATTN_LAYERS)]
    y_s, sc_s, sf_s, k_s, v_s = run_trunk(x_sample, state_conv, state_ffn, k_past, v_past, past_len, *weights)
    return (y_p, y_s, sc_p, sf_p, k_p, v_p, sc_s, sf_s, k_s, v_s)
```

```python
import functools

import jax
import jax.numpy as jnp
from jax import lax
from jax.experimental import pallas as pl
from jax.experimental.pallas import tpu as pltpu

F32 = jnp.float32
BF16 = jnp.bfloat16

D_MODEL = 1024
D_FF = 2816
N_HEADS = 16
HEAD_DIM = 64
CONV_WIDTH = 31
FFN_CONV_WIDTH = 3
PAGE_SIZE = 128
EPS = 1e-6

SUBLANES = 8
LANES = 128
VMEM_BYTES_V7X = 64 * 1024 * 1024
VMEM_LIMIT = VMEM_BYTES_V7X - 8 * 1024 * 1024

HEADS_PER_BLOCK = LANES // HEAD_DIM
ATT_BLOCK = 256
PAGES_PER_STEP = 4
NEW_ROWS = 16


def _params(semantics):
    return pltpu.CompilerParams(dimension_semantics=semantics, vmem_limit_bytes=VMEM_LIMIT)


def _resident(shape, index_map):
    return pl.BlockSpec(shape, index_map, pipeline_mode=pl.Buffered(1))


def _rms(x, g):
    return x * lax.rsqrt(jnp.mean(x * x, axis=-1, keepdims=True) + EPS) * g


def _softplus(z):
    return jnp.maximum(z, 0.0) + jnp.log(1.0 + jnp.exp(-jnp.abs(z)))


def _silu(x):
    return x * jax.nn.sigmoid(x)


def _dot(a, b):
    return jnp.dot(a, b, preferred_element_type=F32)


def _dot_nt(a, b):
    return lax.dot_general(a, b, (((1,), (1,)), ((), ())), preferred_element_type=F32)


def _dot_tn(a, b):
    return lax.dot_general(a, b, (((0,), (0,)), ((), ())), preferred_element_type=F32)


def _glu_conv_kernel(x_ref, g_ref, wa_ref, wg_ref, ba_ref, bg_ref, wdw_ref, bdw_ref, init_ref,
                     y_ref, st_ref, ext_ref, *, shift, tm, n_tiles, row_chunk):
    t = pl.program_id(1)
    c = pl.program_id(2)
    past = (CONV_WIDTH - 1) * shift
    n_copies = ext_ref.shape[0]

    @pl.when(t == 0)
    def _():
        for r in range(n_copies):
            ext_ref[r, c, 0:past - r, :] = init_ref[0, r:past, :]

    h = _rms(x_ref[0], g_ref[...]).astype(BF16)
    a = _dot(h, wa_ref[...]) + ba_ref[...]
    gate = _dot(h, wg_ref[...]) + bg_ref[...]
    glu = a * jax.nn.sigmoid(gate)
    for r in range(n_copies):
        ext_ref[r, c, past - r:past - r + tm, :] = glu

    cb = y_ref.shape[-1]

    def conv_rows(i, carry):
        r0 = pl.multiple_of(i * row_chunk, row_chunk)
        acc = jnp.broadcast_to(bdw_ref[...], (row_chunk, cb))
        for j in range(CONV_WIDTH):
            r = (j * shift) % n_copies
            acc = acc + wdw_ref[j:j + 1, :] * ext_ref[r, c, pl.ds(r0 + (j * shift - r), row_chunk), :]
        y_ref[0, pl.ds(r0, row_chunk), :] = acc
        return carry

    lax.fori_loop(0, tm // row_chunk, conv_rows, 0)

    @pl.when(t == n_tiles - 1)
    def _():
        st_ref[0] = ext_ref[0, c, tm:tm + past, :]

    if n_tiles > 1:
        for r in range(n_copies):
            ext_ref[r, c, 0:past - r, :] = ext_ref[r, c, tm:tm + past - r, :]


def _glu_conv(x, g, w_in, b_in, w_dw, b_dw, init, *, shift, tm, cb, row_chunk, init_per_seq):
    bsz, t_len, d = x.shape
    n_tiles = t_len // tm
    nc = d // cb
    past = (CONV_WIDTH - 1) * shift
    n_copies = 1 if shift % SUBLANES == 0 else SUBLANES
    assert shift == 1 or shift % SUBLANES == 0
    assert t_len % tm == 0 and d % cb == 0 and tm % row_chunk == 0 and row_chunk % SUBLANES == 0
    assert nc == 1 or n_tiles == 1
    assert n_tiles == 1 or past <= tm
    init_map = (lambda b, t, c: (b, 0, c)) if init_per_seq else (lambda b, t, c: (0, 0, c))
    kern = functools.partial(_glu_conv_kernel, shift=shift, tm=tm, n_tiles=n_tiles, row_chunk=row_chunk)
    return pl.pallas_call(
        kern,
        grid=(bsz, n_tiles, nc),
        in_specs=[
            pl.BlockSpec((1, tm, d), lambda b, t, c: (b, t, 0)),
            pl.BlockSpec((1, d), lambda b, t, c: (0, 0)),
            pl.BlockSpec((d, cb), lambda b, t, c: (0, c)),
            pl.BlockSpec((d, cb), lambda b, t, c: (0, nc + c)),
            pl.BlockSpec((1, cb), lambda b, t, c: (0, c)),
            pl.BlockSpec((1, cb), lambda b, t, c: (0, nc + c)),
            pl.BlockSpec((CONV_WIDTH, cb), lambda b, t, c: (0, c)),
            pl.BlockSpec((1, cb), lambda b, t, c: (0, c)),
            pl.BlockSpec((1, past, cb), init_map),
        ],
        out_specs=[
            pl.BlockSpec((1, tm, cb), lambda b, t, c: (b, t, c)),
            pl.BlockSpec((1, past, cb), lambda b, t, c: (b, 0, c)),
        ],
        out_shape=[
            jax.ShapeDtypeStruct((bsz, t_len, d), F32),
            jax.ShapeDtypeStruct((bsz, past, d), F32),
        ],
        scratch_shapes=[pltpu.VMEM((n_copies, nc, past + tm, cb), F32)],
        compiler_params=_params(("arbitrary", "arbitrary", "arbitrary")),
        name="glu_conv",
    )(x, g, w_in, w_in, b_in, b_in, w_dw, b_dw, init)


def _ln_out_kernel(y_ref, x_ref, lg_ref, lb_ref, w_ref, b_ref, o_ref):
    y = y_ref[...]
    mu = jnp.mean(y, axis=-1, keepdims=True)
    yc = y - mu
    var = jnp.mean(yc * yc, axis=-1, keepdims=True)
    z = _silu(yc * lax.rsqrt(var + EPS) * lg_ref[...] + lb_ref[...])
    o_ref[...] = x_ref[...] + _dot(z.astype(BF16), w_ref[...]) + b_ref[...]


def _ln_out(y, x, ln_g, ln_b, w_out, b_out, *, tm):
    m, d = x.shape
    return pl.pallas_call(
        _ln_out_kernel,
        grid=(m // tm,),
        in_specs=[
            pl.BlockSpec((tm, d), lambda i: (i, 0)),
            pl.BlockSpec((tm, d), lambda i: (i, 0)),
            pl.BlockSpec((1, d), lambda i: (0, 0)),
            pl.BlockSpec((1, d), lambda i: (0, 0)),
            _resident((d, d), lambda i: (0, 0)),
            pl.BlockSpec((1, d), lambda i: (0, 0)),
        ],
        out_specs=pl.BlockSpec((tm, d), lambda i: (i, 0)),
        out_shape=jax.ShapeDtypeStruct((m, d), F32),
        compiler_params=_params(("arbitrary",)),
        name="ln_out",
    )(y, x, ln_g, ln_b, w_out, b_out)


def _ffn_kernel(x_ref, g_ref, wg_ref, wu_ref, wdw_ref, bdw_ref, wd_ref, init_ref, gf_ref,
                o_ref, st_ref, ext_ref, h_ref, a_ref, *, shift, tm, n_tiles, fc, final_norm):
    t = pl.program_id(1)
    past = (FFN_CONV_WIDTH - 1) * shift
    off = (-past) % SUBLANES
    new = off + past

    @pl.when(t == 0)
    def _():
        ext_ref[off:new, :] = init_ref[0]

    x = x_ref[0]
    h_ref[...] = _rms(x, g_ref[...]).astype(BF16)
    for k in range(D_FF // fc):
        cs = slice(k * fc, (k + 1) * fc)
        ext_ref[new:new + tm, cs] = _dot(h_ref[...], wg_ref[:, cs])
        gc = bdw_ref[:, cs]
        for j in range(FFN_CONV_WIDTH):
            gc = gc + wdw_ref[j:j + 1, cs] * ext_ref[off + j * shift:off + j * shift + tm, cs]
        up = _dot(h_ref[...], wu_ref[:, cs])
        a_ref[:, cs] = (_silu(gc) * up).astype(BF16)
    out = x + _dot(a_ref[...], wd_ref[...])
    if final_norm:
        out = _rms(out, gf_ref[...])
    o_ref[0] = out

    @pl.when(t == n_tiles - 1)
    def _():
        st_ref[0] = ext_ref[off + tm:off + tm + past, :]

    if n_tiles > 1:
        ext_ref[off:new, :] = ext_ref[off + tm:off + tm + past, :]


def _ffn(x, g, w_gate, w_up, w_dw, b_dw, w_down, init, g_final, *, shift, tm, init_per_seq, final_norm):
    bsz, t_len, d = x.shape
    f = w_gate.shape[1]
    n_tiles = t_len // tm
    past = (FFN_CONV_WIDTH - 1) * shift
    off = (-past) % SUBLANES
    assert t_len % tm == 0 and (n_tiles == 1 or past <= tm)
    init_map = (lambda b, t: (b, 0, 0)) if init_per_seq else (lambda b, t: (0, 0, 0))
    kern = functools.partial(_ffn_kernel, shift=shift, tm=tm, n_tiles=n_tiles, fc=2 * LANES,
                             final_norm=final_norm)
    return pl.pallas_call(
        kern,
        grid=(bsz, n_tiles),
        in_specs=[
            pl.BlockSpec((1, tm, d), lambda b, t: (b, t, 0)),
            pl.BlockSpec((1, d), lambda b, t: (0, 0)),
            _resident((d, f), lambda b, t: (0, 0)),
            _resident((d, f), lambda b, t: (0, 0)),
            pl.BlockSpec((FFN_CONV_WIDTH, f), lambda b, t: (0, 0)),
            pl.BlockSpec((1, f), lambda b, t: (0, 0)),
            _resident((f, d), lambda b, t: (0, 0)),
            pl.BlockSpec((1, past, f), init_map),
            pl.BlockSpec((1, d), lambda b, t: (0, 0)),
        ],
        out_specs=[
            pl.BlockSpec((1, tm, d), lambda b, t: (b, t, 0)),
            pl.BlockSpec((1, past, f), lambda b, t: (b, 0, 0)),
        ],
        out_shape=[
            jax.ShapeDtypeStruct((bsz, t_len, d), F32),
            jax.ShapeDtypeStruct((bsz, past, f), F32),
        ],
        scratch_shapes=[
            pltpu.VMEM((off + past + tm, f), F32),
            pltpu.VMEM((tm, d), BF16),
            pltpu.VMEM((tm, f), BF16),
        ],
        compiler_params=_params(("arbitrary", "arbitrary")),
        name="conv_ffn",
    )(x, g, w_gate, w_up, w_dw, b_dw, w_down, init, g_final)


def _qkv_kernel(x_ref, g_ref, w_ref, q_ref, k_ref, v_ref):
    d = x_ref.shape[-1]
    h = _rms(x_ref[...], g_ref[...]).astype(BF16)
    q_ref[...] = (_dot(h, w_ref[:, 0:d]) * (HEAD_DIM ** -0.5)).astype(q_ref.dtype)
    k_ref[...] = _dot(h, w_ref[:, d:2 * d])
    v_ref[...] = _dot(h, w_ref[:, 2 * d:3 * d])


def _qkv(x, g, w_qkv, *, tm, q_dtype):
    m, d = x.shape
    row = pl.BlockSpec((tm, d), lambda i: (i, 0))
    return pl.pallas_call(
        _qkv_kernel,
        grid=(m // tm,),
        in_specs=[row, pl.BlockSpec((1, d), lambda i: (0, 0)), _resident((d, 3 * d), lambda i: (0, 0))],
        out_specs=[row, row, row],
        out_shape=[
            jax.ShapeDtypeStruct((m, d), q_dtype),
            jax.ShapeDtypeStruct((m, d), F32),
            jax.ShapeDtypeStruct((m, d), F32),
        ],
        compiler_params=_params(("arbitrary",)),
        name="qkv_proj",
    )(x, g, w_qkv)


def _prompt_attn_kernel(bias_ref, q_ref, k_ref, v_ref, u_ref, o_ref, k0_ref, k1_ref, v0_ref, v1_ref):
    hp = pl.program_id(1)
    qi = pl.program_id(2)
    t_len = k_ref.shape[1]
    blk = ATT_BLOCK

    @pl.when(qi == 0)
    def _():
        lane = lax.broadcasted_iota(jnp.int32, (blk, LANES), 1)
        first = lane < HEAD_DIM
        for r in range(t_len // blk):
            rows = slice(r * blk, (r + 1) * blk)
            k = k_ref[0, rows, :]
            v = v_ref[0, rows, :]
            k0_ref[rows, :] = jnp.where(first, k, 0.0).astype(BF16)
            k1_ref[rows, :] = jnp.where(first, 0.0, k).astype(BF16)
            v0_ref[rows, :] = jnp.where(first, v, 0.0).astype(BF16)
            v1_ref[rows, :] = jnp.where(first, 0.0, v).astype(BF16)

    q = q_ref[0]
    row = lax.broadcasted_iota(jnp.int32, (blk, blk), 0)
    col = lax.broadcasted_iota(jnp.int32, (blk, blk), 1)
    causal = col < row

    def block(kb, run, acc, kh_ref, vh_ref, bias, diagonal):
        rows = pl.ds(pl.multiple_of(kb * blk, blk), blk)
        z = _dot_nt(q, kh_ref[rows, :]) + bias
        sp = _softplus(z)
        if diagonal:
            sp = jnp.where(causal, sp, 0.0)
        csum = _dot(sp.astype(BF16), u_ref[...])
        w = jnp.exp(z - csum - run)
        if diagonal:
            w = jnp.where(causal, w, 0.0)
        acc = acc + _dot(w.astype(BF16), vh_ref[rows, :])
        return run + csum[:, 0:1], acc

    acc = jnp.zeros((blk, LANES), F32)
    for hh, (kh_ref, vh_ref) in enumerate(((k0_ref, v0_ref), (k1_ref, v1_ref))):
        bias = bias_ref[hp * HEADS_PER_BLOCK + hh]
        run, acc = block(qi, jnp.zeros((blk, 1), F32), acc, kh_ref, vh_ref, bias, True)

        def body(it, carry, kh_ref=kh_ref, vh_ref=vh_ref, bias=bias):
            return block(qi - 1 - it, carry[0], carry[1], kh_ref, vh_ref, bias, False)

        run, acc = lax.fori_loop(0, qi, body, (run, acc))
    o_ref[0] = acc.astype(o_ref.dtype)


def _prompt_attn(q, k, v, bias, u):
    bsz, t_len, d = k.shape
    blk = ATT_BLOCK
    return pl.pallas_call(
        _prompt_attn_kernel,
        grid=(bsz, d // LANES, t_len // blk),
        in_specs=[
            pl.BlockSpec(memory_space=pltpu.SMEM),
            pl.BlockSpec((1, blk, LANES), lambda b, h, i: (b, i, h)),
            pl.BlockSpec((1, t_len, LANES), lambda b, h, i: (b, 0, h)),
            pl.BlockSpec((1, t_len, LANES), lambda b, h, i: (b, 0, h)),
            pl.BlockSpec((blk, blk), lambda b, h, i: (0, 0)),
        ],
        out_specs=pl.BlockSpec((1, blk, LANES), lambda b, h, i: (b, i, h)),
        out_shape=jax.ShapeDtypeStruct((bsz, t_len, d), BF16),
        scratch_shapes=[pltpu.VMEM((t_len, LANES), BF16)] * 4,
        compiler_params=_params(("arbitrary", "arbitrary", "arbitrary")),
        name="prompt_attn",
    )(bias, q, k, v, u)


def _sample_attn_kernel(pt_ref, q_ref, kn_ref, vn_ref, bias_ref, u_ref, *rest):
    n = PAGES_PER_STEP
    k_pages, v_pages = rest[:n], rest[n:2 * n]
    o_ref, qm_ref, run_ref, acc_ref = rest[2 * n:]
    g = pl.program_id(1)
    n_new = q_ref.shape[1]
    d = q_ref.shape[2]
    cols = n_new * N_HEADS

    lane_head = lax.broadcasted_iota(jnp.int32, (N_HEADS, d), 1) // HEAD_DIM
    own_head = lane_head == lax.broadcasted_iota(jnp.int32, (N_HEADS, d), 0)

    def block(k, v, valid):
        rows = k.shape[0]
        z = _dot_nt(k.astype(BF16), qm_ref[...]) + bias_ref[...]
        sp = _softplus(z)
        if valid is not None:
            sp = jnp.where(valid, sp, 0.0)
        csum = _dot(u_ref[0:rows, 0:rows], sp.astype(BF16))
        w = jnp.exp(z - csum - run_ref[...])
        if valid is not None:
            w = jnp.where(valid, w, 0.0)
        acc_ref[...] += _dot_tn(w.astype(BF16), v.astype(BF16))
        run_ref[...] += csum[0:1, :]

    @pl.when(g == 0)
    def _():
        q = q_ref[0]
        qm_ref[...] = jnp.concatenate(
            [jnp.where(own_head, jnp.broadcast_to(q[t:t + 1, :], (N_HEADS, d)), 0.0) for t in range(n_new)],
            axis=0).astype(BF16)
        run_ref[...] = jnp.zeros_like(run_ref)
        acc_ref[...] = jnp.zeros_like(acc_ref)
        key = lax.broadcasted_iota(jnp.int32, (NEW_ROWS, cols), 0)
        qry = lax.broadcasted_iota(jnp.int32, (NEW_ROWS, cols), 1) // N_HEADS
        block(kn_ref[0], vn_ref[0], key < qry)

    for s in range(n):
        block(k_pages[s][0], v_pages[s][0], None)

    @pl.when(g == pl.num_programs(1) - 1)
    def _():
        for t in range(n_new):
            part = acc_ref[t * N_HEADS:(t + 1) * N_HEADS, :]
            o_ref[0, t:t + 1, :] = jnp.sum(jnp.where(own_head, part, 0.0), axis=0, keepdims=True)


def _sample_attn(q, k_new, v_new, bias_cols, u, cache_k, cache_v, page_table):
    bsz, n_new, d = q.shape
    n_pages = page_table.shape[1]
    n = PAGES_PER_STEP
    assert n_pages % n == 0
    cols = n_new * N_HEADS

    def page_map(s):
        return lambda b, g, pt: (pt[b * n_pages + (n_pages - 1 - (g * n + s))], 0, 0)

    page_specs = lambda: [pl.BlockSpec((1, PAGE_SIZE, d), page_map(s)) for s in range(n)]
    grid_spec = pltpu.PrefetchScalarGridSpec(
        num_scalar_prefetch=1,
        grid=(bsz, n_pages // n),
        in_specs=[
            pl.BlockSpec((1, n_new, d), lambda b, g, pt: (b, 0, 0)),
            pl.BlockSpec((1, NEW_ROWS, d), lambda b, g, pt: (b, 0, 0)),
            pl.BlockSpec((1, NEW_ROWS, d), lambda b, g, pt: (b, 0, 0)),
            pl.BlockSpec((1, cols), lambda b, g, pt: (0, 0)),
            pl.BlockSpec((PAGE_SIZE, PAGE_SIZE), lambda b, g, pt: (0, 0)),
        ] + page_specs() + page_specs(),
        out_specs=pl.BlockSpec((1, n_new, d), lambda b, g, pt: (b, 0, 0)),
        scratch_shapes=[
            pltpu.VMEM((cols, d), BF16),
            pltpu.VMEM((1, cols), F32),
            pltpu.VMEM((cols, d), F32),
        ],
    )
    return pl.pallas_call(
        _sample_attn_kernel,
        grid_spec=grid_spec,
        out_shape=jax.ShapeDtypeStruct((bsz, n_new, d), F32),
        compiler_params=_params(("arbitrary", "arbitrary")),
        name="sample_attn",
    )(page_table.reshape(-1), q, k_new, v_new, bias_cols, u, *([cache_k] * n), *([cache_v] * n))


def _out_proj_kernel(a_ref, x_ref, w_ref, o_ref):
    o_ref[...] = x_ref[...] + _dot(a_ref[...], w_ref[...])


def _out_proj(att, x, w_o, *, tm):
    m, d = x.shape
    row = pl.BlockSpec((tm, d), lambda i: (i, 0))
    return pl.pallas_call(
        _out_proj_kernel,
        grid=(m // tm,),
        in_specs=[row, row, _resident((d, d), lambda i: (0, 0))],
        out_specs=row,
        out_shape=jax.ShapeDtypeStruct((m, d), F32),
        compiler_params=_params(("arbitrary",)),
        name="attn_out_proj",
    )(att, x, w_o)


def _suffix_ones(n):
    i = lax.broadcasted_iota(jnp.int32, (n, n), 0)
    j = lax.broadcasted_iota(jnp.int32, (n, n), 1)
    return i, j


def kernel(x_prompt, x_sample, state_conv, state_ffn, cache_k, cache_v, page_table, norm_mix, norm_ffn, norm_final, conv_w_in, conv_b_in, conv_w_dw, conv_b_dw, conv_ln_g, conv_ln_b, conv_w_out, conv_b_out, attn_w_qkv, attn_w_o, attn_b_logit, ffn_w_gate, ffn_w_up, ffn_w_dw, ffn_b_dw, ffn_w_down):
    d = D_MODEL
    bsz, seq, _ = x_prompt.shape
    dbs, dseq, _ = x_sample.shape
    row2 = lambda a: a.reshape(1, -1)

    w_in = conv_w_in[0].astype(BF16)
    w_out = conv_w_out[0].astype(BF16)
    w_qkv = attn_w_qkv[0].astype(BF16)
    w_o = attn_w_o[0].astype(BF16)
    w_gate = ffn_w_gate.astype(BF16)
    w_up = ffn_w_up.astype(BF16)
    w_down = ffn_w_down.astype(BF16)
    bias = attn_b_logit[0]
    g_final = row2(norm_final)

    i, j = _suffix_ones(ATT_BLOCK)
    u_row = (i >= j).astype(BF16)
    i, j = _suffix_ones(PAGE_SIZE)
    u_col = (j >= i).astype(BF16)

    def layer0(x, conv_init, ffn_init, *, shift, tm, cb, row_chunk, per_seq):
        b, t, _ = x.shape
        y, conv_st = _glu_conv(x, row2(norm_mix[0]), w_in, row2(conv_b_in[0]), conv_w_dw[0], row2(conv_b_dw[0]),
                               conv_init, shift=shift, tm=tm, cb=cb, row_chunk=row_chunk, init_per_seq=per_seq)
        x = _ln_out(y.reshape(b * t, d), x.reshape(b * t, d), row2(conv_ln_g[0]), row2(conv_ln_b[0]), w_out,
                    row2(conv_b_out[0]), tm=tm).reshape(b, t, d)
        x, ffn_st = _ffn(x, row2(norm_ffn[0]), w_gate[0], w_up[0], ffn_w_dw[0], row2(ffn_b_dw[0]), w_down[0],
                         ffn_init, g_final, shift=shift, tm=tm, init_per_seq=per_seq, final_norm=False)
        return x, conv_st, ffn_st

    def ffn1(x, ffn_init, *, shift, tm, per_seq):
        return _ffn(x, row2(norm_ffn[1]), w_gate[1], w_up[1], ffn_w_dw[1], row2(ffn_b_dw[1]), w_down[1],
                    ffn_init, g_final, shift=shift, tm=tm, init_per_seq=per_seq, final_norm=True)

    tm = 512
    zero_conv = jnp.zeros((1, CONV_WIDTH - 1, d), F32)
    zero_ffn = jnp.zeros((1, FFN_CONV_WIDTH - 1, D_FF), F32)
    xp, sc_p, sf0_p = layer0(x_prompt, zero_conv, zero_ffn, shift=1, tm=tm, cb=d, row_chunk=16, per_seq=False)
    q, k_p, v_p = _qkv(xp.reshape(bsz * seq, d), row2(norm_mix[1]), w_qkv, tm=tm, q_dtype=BF16)
    att = _prompt_attn(q.reshape(bsz, seq, d), k_p.reshape(bsz, seq, d), v_p.reshape(bsz, seq, d), bias, u_row)
    xp = _out_proj(att.reshape(bsz * seq, d), xp.reshape(bsz * seq, d), w_o, tm=tm).reshape(bsz, seq, d)
    y_p, sf1_p = ffn1(xp, zero_ffn, shift=1, tm=tm, per_seq=False)

    n_tok = dbs * dseq
    to_tm = lambda a: a.transpose(1, 0, 2).reshape(1, a.shape[1] * dbs, a.shape[2])
    from_tm = lambda a: a.reshape(a.shape[1] // dbs, dbs, a.shape[2]).transpose(1, 0, 2)
    xs, sc_s, sf0_s = layer0(to_tm(x_sample), to_tm(state_conv[0]), to_tm(state_ffn[0]),
                             shift=dbs, tm=n_tok, cb=2 * LANES, row_chunk=64, per_seq=True)
    q, k_s, v_s = _qkv(xs.reshape(n_tok, d), row2(norm_mix[1]), w_qkv, tm=n_tok, q_dtype=F32)
    q, k_s, v_s = (from_tm(a.reshape(1, n_tok, d)) for a in (q, k_s, v_s))
    pad = ((0, 0), (0, NEW_ROWS - dseq), (0, 0))
    n_pool = cache_k.shape[1]
    att = _sample_attn(q, jnp.pad(k_s, pad), jnp.pad(v_s, pad), row2(jnp.tile(bias, dseq)), u_col,
                       cache_k[0].reshape(n_pool, PAGE_SIZE, d), cache_v[0].reshape(n_pool, PAGE_SIZE, d),
                       page_table)
    xs = _out_proj(to_tm(att).reshape(n_tok, d).astype(BF16), xs.reshape(n_tok, d), w_o, tm=n_tok)
    y_s, sf1_s = ffn1(xs.reshape(1, n_tok, d), to_tm(state_ffn[1]), shift=dbs, tm=n_tok, per_seq=True)

    heads = lambda a, b, t: a.reshape(1, b, t, N_HEADS, HEAD_DIM)
    return (
        y_p,
        from_tm(y_s),
        sc_p[None],
        jnp.stack([sf0_p, sf1_p]),
        heads(k_p, bsz, seq),
        heads(v_p, bsz, seq),
        from_tm(sc_s)[None],
        jnp.stack([from_tm(sf0_s), from_tm(sf1_s)]),
        heads(k_s, dbs, dseq),
        heads(v_s, dbs, dseq),
    )
```

```python
import functools

import jax
import jax.numpy as jnp
from jax import lax
from jax.experimental import pallas as pl
from jax.experimental.pallas import tpu as pltpu

F32 = jnp.float32
BF16 = jnp.bfloat16

D_MODEL = 1024
D_FF = 2816
N_HEADS = 16
HEAD_DIM = 64
CONV_WIDTH = 31
FFN_CONV_WIDTH = 3
PAGE_SIZE = 128
EPS = 1e-6

SUBLANES = 8
LANES = 128
VMEM_BYTES_V7X = 64 * 1024 * 1024
VMEM_LIMIT = VMEM_BYTES_V7X - 8 * 1024 * 1024

HEADS_PER_BLOCK = LANES // HEAD_DIM
ATT_Q_BLOCK = 512
ATT_K_BLOCK = 256
PAGES_PER_STEP = 8
NEW_ROWS = 16
LOG2E = 1.4426950408889634


def _params(semantics):
    return pltpu.CompilerParams(dimension_semantics=semantics, vmem_limit_bytes=VMEM_LIMIT)


def _resident(shape, index_map):
    return pl.BlockSpec(shape, index_map, pipeline_mode=pl.Buffered(1))


def _rms(x, g):
    return x * lax.rsqrt(jnp.mean(x * x, axis=-1, keepdims=True) + EPS) * g


def _softplus2(z):
    return jnp.maximum(z, 0.0) + jnp.log(1.0 + jnp.exp2(-jnp.abs(z))) * LOG2E


def _silu(x):
    return x * jax.nn.sigmoid(x)


def _dot(a, b):
    return jnp.dot(a, b, preferred_element_type=F32)


def _dot_nt(a, b):
    return lax.dot_general(a, b, (((1,), (1,)), ((), ())), preferred_element_type=F32)


def _glu_conv_kernel(x_ref, g_ref, wa_ref, wg_ref, ba_ref, bg_ref, wdw_ref, bdw_ref, init_ref,
                     y_ref, st_ref, ext_ref, *, shift, tm, n_tiles, row_chunk):
    t = pl.program_id(1)
    c = pl.program_id(2)
    past = (CONV_WIDTH - 1) * shift
    n_copies = ext_ref.shape[0]

    @pl.when(t == 0)
    def _():
        for r in range(n_copies):
            ext_ref[r, c, 0:past - r, :] = init_ref[0, r:past, :]

    h = _rms(x_ref[0], g_ref[...]).astype(BF16)
    a = _dot(h, wa_ref[...]) + ba_ref[...]
    gate = _dot(h, wg_ref[...]) + bg_ref[...]
    glu = a * jax.nn.sigmoid(gate)
    for r in range(n_copies):
        ext_ref[r, c, past - r:past - r + tm, :] = glu

    cb = y_ref.shape[-1]

    def conv_rows(i, carry):
        r0 = pl.multiple_of(i * row_chunk, row_chunk)
        acc = jnp.broadcast_to(bdw_ref[...], (row_chunk, cb))
        for j in range(CONV_WIDTH):
            r = (j * shift) % n_copies
            acc = acc + wdw_ref[j:j + 1, :] * ext_ref[r, c, pl.ds(r0 + (j * shift - r), row_chunk), :]
        y_ref[0, pl.ds(r0, row_chunk), :] = acc
        return carry

    lax.fori_loop(0, tm // row_chunk, conv_rows, 0)

    @pl.when(t == n_tiles - 1)
    def _():
        st_ref[0] = ext_ref[0, c, tm:tm + past, :]

    if n_tiles > 1:
        for r in range(n_copies):
            ext_ref[r, c, 0:past - r, :] = ext_ref[r, c, tm:tm + past - r, :]


def _glu_conv(x, g, w_in, b_in, w_dw, b_dw, init, *, shift, tm, cb, row_chunk):
    bsz, t_len, d = x.shape
    n_tiles = t_len // tm
    nc = d // cb
    past = (CONV_WIDTH - 1) * shift
    n_copies = 1 if shift % SUBLANES == 0 else SUBLANES
    assert shift == 1 or shift % SUBLANES == 0
    assert t_len % tm == 0 and d % cb == 0 and tm % row_chunk == 0 and row_chunk % SUBLANES == 0
    assert nc == 1 or n_tiles == 1
    assert n_tiles == 1 or past <= tm
    kern = functools.partial(_glu_conv_kernel, shift=shift, tm=tm, n_tiles=n_tiles, row_chunk=row_chunk)
    return pl.pallas_call(
        kern,
        grid=(bsz, n_tiles, nc),
        in_specs=[
            pl.BlockSpec((1, tm, d), lambda b, t, c: (b, t, 0)),
            pl.BlockSpec((1, d), lambda b, t, c: (0, 0)),
            pl.BlockSpec((d, cb), lambda b, t, c: (0, c)),
            pl.BlockSpec((d, cb), lambda b, t, c: (0, nc + c)),
            pl.BlockSpec((1, cb), lambda b, t, c: (0, c)),
            pl.BlockSpec((1, cb), lambda b, t, c: (0, nc + c)),
            pl.BlockSpec((CONV_WIDTH, cb), lambda b, t, c: (0, c)),
            pl.BlockSpec((1, cb), lambda b, t, c: (0, c)),
            pl.BlockSpec((1, past, cb), lambda b, t, c: (0, 0, c)),
        ],
        out_specs=[
            pl.BlockSpec((1, tm, cb), lambda b, t, c: (b, t, c)),
            pl.BlockSpec((1, past, cb), lambda b, t, c: (b, 0, c)),
        ],
        out_shape=[
            jax.ShapeDtypeStruct((bsz, t_len, d), F32),
            jax.ShapeDtypeStruct((bsz, past, d), F32),
        ],
        scratch_shapes=[pltpu.VMEM((n_copies, nc, past + tm, cb), F32)],
        compiler_params=_params(("arbitrary", "arbitrary", "arbitrary")),
        name="glu_conv",
    )(x, g, w_in, w_in, b_in, b_in, w_dw, b_dw, init)


def _ln_out_kernel(y_ref, x_ref, lg_ref, lb_ref, w_ref, b_ref, o_ref):
    y = y_ref[...]
    mu = jnp.mean(y, axis=-1, keepdims=True)
    yc = y - mu
    var = jnp.mean(yc * yc, axis=-1, keepdims=True)
    z = _silu(yc * lax.rsqrt(var + EPS) * lg_ref[...] + lb_ref[...])
    o_ref[...] = x_ref[...] + _dot(z.astype(BF16), w_ref[...]) + b_ref[...]


def _ln_out(y, x, ln_g, ln_b, w_out, b_out, *, tm):
    m, d = x.shape
    return pl.pallas_call(
        _ln_out_kernel,
        grid=(m // tm,),
        in_specs=[
            pl.BlockSpec((tm, d), lambda i: (i, 0)),
            pl.BlockSpec((tm, d), lambda i: (i, 0)),
            pl.BlockSpec((1, d), lambda i: (0, 0)),
            pl.BlockSpec((1, d), lambda i: (0, 0)),
            _resident((d, d), lambda i: (0, 0)),
            pl.BlockSpec((1, d), lambda i: (0, 0)),
        ],
        out_specs=pl.BlockSpec((tm, d), lambda i: (i, 0)),
        out_shape=jax.ShapeDtypeStruct((m, d), F32),
        compiler_params=_params(("arbitrary",)),
        name="ln_out",
    )(y, x, ln_g, ln_b, w_out, b_out)


def _ffn_kernel(x_ref, g_ref, wg_ref, wu_ref, wdw_ref, bdw_ref, wd_ref, init_ref, gf_ref,
                o_ref, st_ref, ext_ref, h_ref, a_ref, *, shift, tm, n_tiles, fc, final_norm):
    t = pl.program_id(1)
    past = (FFN_CONV_WIDTH - 1) * shift
    off = (-past) % SUBLANES
    new = off + past

    @pl.when(t == 0)
    def _():
        ext_ref[off:new, :] = init_ref[0]

    x = x_ref[0]
    h_ref[...] = _rms(x, g_ref[...]).astype(BF16)
    for k in range(D_FF // fc):
        cs = slice(k * fc, (k + 1) * fc)
        ext_ref[new:new + tm, cs] = _dot(h_ref[...], wg_ref[:, cs])
        gc = bdw_ref[:, cs]
        for j in range(FFN_CONV_WIDTH):
            gc = gc + wdw_ref[j:j + 1, cs] * ext_ref[off + j * shift:off + j * shift + tm, cs]
        up = _dot(h_ref[...], wu_ref[:, cs])
        a_ref[:, cs] = (_silu(gc) * up).astype(BF16)
    out = x + _dot(a_ref[...], wd_ref[...])
    if final_norm:
        out = _rms(out, gf_ref[...])
    o_ref[0] = out

    @pl.when(t == n_tiles - 1)
    def _():
        st_ref[0] = ext_ref[off + tm:off + tm + past, :]

    if n_tiles > 1:
        ext_ref[off:new, :] = ext_ref[off + tm:off + tm + past, :]


def _ffn(x, g, w_gate, w_up, w_dw, b_dw, w_down, init, g_final, *, shift, tm, final_norm):
    bsz, t_len, d = x.shape
    f = w_gate.shape[1]
    n_tiles = t_len // tm
    past = (FFN_CONV_WIDTH - 1) * shift
    off = (-past) % SUBLANES
    assert t_len % tm == 0 and (n_tiles == 1 or past <= tm)
    kern = functools.partial(_ffn_kernel, shift=shift, tm=tm, n_tiles=n_tiles, fc=2 * LANES,
                             final_norm=final_norm)
    return pl.pallas_call(
        kern,
        grid=(bsz, n_tiles),
        in_specs=[
            pl.BlockSpec((1, tm, d), lambda b, t: (b, t, 0)),
            pl.BlockSpec((1, d), lambda b, t: (0, 0)),
            _resident((d, f), lambda b, t: (0, 0)),
            _resident((d, f), lambda b, t: (0, 0)),
            pl.BlockSpec((FFN_CONV_WIDTH, f), lambda b, t: (0, 0)),
            pl.BlockSpec((1, f), lambda b, t: (0, 0)),
            _resident((f, d), lambda b, t: (0, 0)),
            pl.BlockSpec((1, past, f), lambda b, t: (0, 0, 0)),
            pl.BlockSpec((1, d), lambda b, t: (0, 0)),
        ],
        out_specs=[
            pl.BlockSpec((1, tm, d), lambda b, t: (b, t, 0)),
            pl.BlockSpec((1, past, f), lambda b, t: (b, 0, 0)),
        ],
        out_shape=[
            jax.ShapeDtypeStruct((bsz, t_len, d), F32),
            jax.ShapeDtypeStruct((bsz, past, f), F32),
        ],
        scratch_shapes=[
            pltpu.VMEM((off + past + tm, f), F32),
            pltpu.VMEM((tm, d), BF16),
            pltpu.VMEM((tm, f), BF16),
        ],
        compiler_params=_params(("arbitrary", "arbitrary")),
        name="conv_ffn",
    )(x, g, w_gate, w_up, w_dw, b_dw, w_down, init, g_final)


Q_SCALE = LOG2E * HEAD_DIM ** -0.5


def _qkv_kernel(x_ref, g_ref, w_ref, q_ref, k_ref, v_ref):
    d = x_ref.shape[-1]
    h = _rms(x_ref[...], g_ref[...]).astype(BF16)
    q_ref[...] = _dot(h, w_ref[:, 0:d]) * Q_SCALE
    k_ref[...] = _dot(h, w_ref[:, d:2 * d])
    v_ref[...] = _dot(h, w_ref[:, 2 * d:3 * d])


def _qkv(x, g, w_qkv, *, tm):
    m, d = x.shape
    row = pl.BlockSpec((tm, d), lambda i: (i, 0))
    return pl.pallas_call(
        _qkv_kernel,
        grid=(m // tm,),
        in_specs=[row, pl.BlockSpec((1, d), lambda i: (0, 0)), _resident((d, 3 * d), lambda i: (0, 0))],
        out_specs=[row, row, row],
        out_shape=[jax.ShapeDtypeStruct((m, d), F32)] * 3,
        compiler_params=_params(("arbitrary",)),
        name="qkv_proj",
    )(x, g, w_qkv)


def _qkv_t_kernel(x_ref, g_ref, wq_ref, wv_ref, wkt_ref, wvt_ref, q_ref, v_ref, kt_ref, vt_ref, ktb_ref):
    h = _rms(x_ref[0], g_ref[...]).astype(BF16)
    q_ref[0] = (_dot(h, wq_ref[...]) * Q_SCALE).astype(BF16)
    v_ref[0] = _dot(h, wv_ref[...]).astype(BF16)
    kt = _dot_nt(wkt_ref[...], h)
    kt_ref[0] = kt
    ktb_ref[0] = kt.astype(BF16)
    vt_ref[0] = _dot_nt(wvt_ref[...], h)


def _qkv_t(x, g, w_q, w_v, w_kt, w_vt, *, tm):
    bsz, t_len, d = x.shape
    row = pl.BlockSpec((1, tm, d), lambda b, t: (b, t, 0))
    col = pl.BlockSpec((1, d, tm), lambda b, t: (b, 0, t))
    weight = lambda: _resident((d, d), lambda b, t: (0, 0))
    return pl.pallas_call(
        _qkv_t_kernel,
        grid=(bsz, t_len // tm),
        in_specs=[row, pl.BlockSpec((1, d), lambda b, t: (0, 0)), weight(), weight(), weight(), weight()],
        out_specs=[row, row, col, col, col],
        out_shape=[
            jax.ShapeDtypeStruct((bsz, t_len, d), BF16),
            jax.ShapeDtypeStruct((bsz, t_len, d), BF16),
            jax.ShapeDtypeStruct((bsz, d, t_len), F32),
            jax.ShapeDtypeStruct((bsz, d, t_len), F32),
            jax.ShapeDtypeStruct((bsz, d, t_len), BF16),
        ],
        compiler_params=_params(("arbitrary", "arbitrary")),
        name="qkv_proj_t",
    )(x, g, w_q, w_v, w_kt, w_vt)


def _prompt_attn_kernel(bias_ref, q_ref, kt_ref, v_ref, u_ref, o_ref):
    hp = pl.program_id(1)
    qi = pl.program_id(2)
    tq = q_ref.shape[1]
    kb = ATT_K_BLOCK
    n_sub = tq // kb

    q = q_ref[0]
    first = lax.broadcasted_iota(jnp.int32, (tq, LANES), 1) < HEAD_DIM
    zero = jnp.zeros_like(q)
    q_heads = (jnp.where(first, q, zero), jnp.where(first, zero, q))
    biases = [bias_ref[hp * HEADS_PER_BLOCK + hh] for hh in range(HEADS_PER_BLOCK)]

    def key_block(col0, carry, diag_offset):
        cols = pl.ds(pl.multiple_of(col0, kb), kb)
        kt = kt_ref[0, :, cols]
        v = v_ref[0, cols, :]
        if diag_offset is not None:
            row = lax.broadcasted_iota(jnp.int32, (tq, kb), 0)
            col = lax.broadcasted_iota(jnp.int32, (tq, kb), 1)
            visible = col + diag_offset < row
        out = []
        for hh in range(HEADS_PER_BLOCK):
            run, acc = carry[hh]
            z = _dot(q_heads[hh], kt) + biases[hh]
            sp = _softplus2(z)
            if diag_offset is not None:
                sp = jnp.where(visible, sp, 0.0)
            csum = _dot(sp.astype(BF16), u_ref[...])
            w = jnp.exp2(z - csum - run)
            if diag_offset is not None:
                w = jnp.where(visible, w, 0.0)
            out.append((run + csum[:, 0:1], acc + _dot(w.astype(BF16), v)))
        return tuple(out)

    carry = tuple((jnp.zeros((tq, 1), F32), jnp.zeros((tq, LANES), F32)) for _ in range(HEADS_PER_BLOCK))
    for s in reversed(range(n_sub)):
        carry = key_block(qi * tq + s * kb, carry, s * kb)

    def body(it, carry):
        base = (qi - 1 - it) * tq
        for s in reversed(range(n_sub)):
            carry = key_block(base + s * kb, carry, None)
        return carry

    carry = lax.fori_loop(0, qi, body, carry)
    o_ref[0] = jnp.where(first, carry[0][1], carry[1][1]).astype(o_ref.dtype)


def _prompt_attn(q, kt, v, bias, u):
    bsz, t_len, d = q.shape
    tq = ATT_Q_BLOCK
    return pl.pallas_call(
        _prompt_attn_kernel,
        grid=(bsz, d // LANES, t_len // tq),
        in_specs=[
            pl.BlockSpec(memory_space=pltpu.SMEM),
            pl.BlockSpec((1, tq, LANES), lambda b, h, i: (b, i, h)),
            pl.BlockSpec((1, LANES, t_len), lambda b, h, i: (b, h, 0)),
            pl.BlockSpec((1, t_len, LANES), lambda b, h, i: (b, 0, h)),
            pl.BlockSpec((ATT_K_BLOCK, ATT_K_BLOCK), lambda b, h, i: (0, 0)),
        ],
        out_specs=pl.BlockSpec((1, tq, LANES), lambda b, h, i: (b, i, h)),
        out_shape=jax.ShapeDtypeStruct((bsz, t_len, d), BF16),
        compiler_params=_params(("arbitrary", "arbitrary", "arbitrary")),
        name="prompt_attn",
    )(bias, q, kt, v, u)


def _sample_attn_kernel(pt_ref, q_ref, kn_ref, vn_ref, bias_ref, u_ref, *rest):
    n = PAGES_PER_STEP
    kt_pages, vt_pages = rest[:n], rest[n:2 * n]
    o_ref, qm_ref, run_ref, acc_ref = rest[2 * n:]
    g = pl.program_id(1)
    n_new = q_ref.shape[1]
    d = q_ref.shape[2]
    rows = n_new * N_HEADS
    step_keys = u_ref.shape[0]

    lane_head = lax.broadcasted_iota(jnp.int32, (N_HEADS, d), 1) // HEAD_DIM
    own_head = lane_head == lax.broadcasted_iota(jnp.int32, (N_HEADS, d), 0)

    def key_block(z, visible, pv):
        keys = z.shape[1]
        z = z + bias_ref[:, 0:keys]
        sp = _softplus2(z)
        if visible is not None:
            sp = jnp.where(visible, sp, 0.0)
        sums = _dot(sp.astype(BF16), u_ref[0:keys, :])
        run = run_ref[...]
        run_keys = run[:, 0:keys] if keys < LANES else jnp.concatenate([run] * (keys // LANES), axis=1)
        w = jnp.exp2(z - sums[:, 0:keys] - run_keys)
        if visible is not None:
            w = jnp.where(visible, w, 0.0)
        acc_ref[...] += pv(w.astype(BF16))
        run_ref[...] = run + sums[:, step_keys:]

    @pl.when(g == 0)
    def _():
        q = q_ref[0]
        qm_ref[...] = jnp.concatenate(
            [jnp.where(own_head, jnp.broadcast_to(q[t:t + 1, :], (N_HEADS, d)), 0.0) for t in range(n_new)],
            axis=0).astype(BF16)
        run_ref[...] = jnp.zeros_like(run_ref)
        acc_ref[...] = jnp.zeros_like(acc_ref)
        key = lax.broadcasted_iota(jnp.int32, (rows, NEW_ROWS), 1)
        qry = lax.broadcasted_iota(jnp.int32, (rows, NEW_ROWS), 0) // N_HEADS
        vn = vn_ref[0].astype(BF16)
        key_block(_dot_nt(qm_ref[...], kn_ref[0].astype(BF16)), key < qry, lambda w: _dot(w, vn))

    kt = jnp.concatenate([p[0].astype(BF16) for p in kt_pages], axis=1)
    vt = jnp.concatenate([p[0].astype(BF16) for p in vt_pages], axis=1)
    key_block(_dot(qm_ref[...], kt), None, lambda w: _dot_nt(w, vt))

    @pl.when(g == pl.num_programs(1) - 1)
    def _():
        for t in range(n_new):
            part = acc_ref[t * N_HEADS:(t + 1) * N_HEADS, :]
            o_ref[0, t:t + 1, :] = jnp.sum(jnp.where(own_head, part, 0.0), axis=0, keepdims=True)


def _sample_attn(q, k_new, v_new, bias_rows, u, cache_kt, cache_vt, page_table):
    bsz, n_new, d = q.shape
    n_pages = page_table.shape[1]
    n = PAGES_PER_STEP
    assert n_pages % n == 0
    rows = n_new * N_HEADS

    def page_map(s):
        return lambda b, g, pt: (pt[b * n_pages + (n_pages - (g + 1) * n + s)], 0, 0)

    page_specs = lambda: [pl.BlockSpec((1, d, PAGE_SIZE), page_map(s)) for s in range(n)]
    grid_spec = pltpu.PrefetchScalarGridSpec(
        num_scalar_prefetch=1,
        grid=(bsz, n_pages // n),
        in_specs=[
            pl.BlockSpec((1, n_new, d), lambda b, g, pt: (b, 0, 0)),
            pl.BlockSpec((1, NEW_ROWS, d), lambda b, g, pt: (b, 0, 0)),
            pl.BlockSpec((1, NEW_ROWS, d), lambda b, g, pt: (b, 0, 0)),
            pl.BlockSpec((rows, n * PAGE_SIZE), lambda b, g, pt: (0, 0)),
            _resident((n * PAGE_SIZE, n * PAGE_SIZE + LANES), lambda b, g, pt: (0, 0)),
        ] + page_specs() + page_specs(),
        out_specs=pl.BlockSpec((1, n_new, d), lambda b, g, pt: (b, 0, 0)),
        scratch_shapes=[
            pltpu.VMEM((rows, d), BF16),
            pltpu.VMEM((rows, LANES), F32),
            pltpu.VMEM((rows, d), F32),
        ],
    )
    return pl.pallas_call(
        _sample_attn_kernel,
        grid_spec=grid_spec,
        out_shape=jax.ShapeDtypeStruct((bsz, n_new, d), F32),
        compiler_params=_params(("arbitrary", "arbitrary")),
        name="sample_attn",
    )(page_table.reshape(-1), q, k_new, v_new, bias_rows, u, *([cache_kt] * n), *([cache_vt] * n))


def _out_proj_kernel(a_ref, x_ref, w_ref, o_ref):
    o_ref[...] = x_ref[...] + _dot(a_ref[...], w_ref[...])


def _out_proj(att, x, w_o, *, tm):
    m, d = x.shape
    row = pl.BlockSpec((tm, d), lambda i: (i, 0))
    return pl.pallas_call(
        _out_proj_kernel,
        grid=(m // tm,),
        in_specs=[row, row, _resident((d, d), lambda i: (0, 0))],
        out_specs=row,
        out_shape=jax.ShapeDtypeStruct((m, d), F32),
        compiler_params=_params(("arbitrary",)),
        name="attn_out_proj",
    )(att, x, w_o)


def _suffix_ones(n):
    return (lax.broadcasted_iota(jnp.int32, (n, n), 0) >= lax.broadcasted_iota(jnp.int32, (n, n), 1)).astype(BF16)


def kernel(x_prompt, x_sample, state_conv, state_ffn, cache_k, cache_v, page_table, norm_mix, norm_ffn, norm_final, conv_w_in, conv_b_in, conv_w_dw, conv_b_dw, conv_ln_g, conv_ln_b, conv_w_out, conv_b_out, attn_w_qkv, attn_w_o, attn_b_logit, ffn_w_gate, ffn_w_up, ffn_w_dw, ffn_b_dw, ffn_w_down):
    d = D_MODEL
    bsz, seq, _ = x_prompt.shape
    dbs, dseq, _ = x_sample.shape
    row2 = lambda a: a.reshape(1, -1)

    w_in = conv_w_in[0].astype(BF16)
    w_out = conv_w_out[0].astype(BF16)
    w_qkv = attn_w_qkv[0].astype(BF16)
    w_o = attn_w_o[0].astype(BF16)
    w_gate = ffn_w_gate.astype(BF16)
    w_up = ffn_w_up.astype(BF16)
    w_down = ffn_w_down.astype(BF16)
    bias = attn_b_logit[0] * LOG2E
    g_final = row2(norm_final)

    def layer0(x, conv_init, ffn_init, *, shift, tm, cb, row_chunk):
        b, t, _ = x.shape
        y, conv_st = _glu_conv(x, row2(norm_mix[0]), w_in, row2(conv_b_in[0]), conv_w_dw[0], row2(conv_b_dw[0]),
                               conv_init, shift=shift, tm=tm, cb=cb, row_chunk=row_chunk)
        x = _ln_out(y.reshape(b * t, d), x.reshape(b * t, d), row2(conv_ln_g[0]), row2(conv_ln_b[0]), w_out,
                    row2(conv_b_out[0]), tm=tm).reshape(b, t, d)
        x, ffn_st = _ffn(x, row2(norm_ffn[0]), w_gate[0], w_up[0], ffn_w_dw[0], row2(ffn_b_dw[0]), w_down[0],
                         ffn_init, g_final, shift=shift, tm=tm, final_norm=False)
        return x, conv_st, ffn_st

    def ffn1(x, ffn_init, *, shift, tm):
        return _ffn(x, row2(norm_ffn[1]), w_gate[1], w_up[1], ffn_w_dw[1], row2(ffn_b_dw[1]), w_down[1],
                    ffn_init, g_final, shift=shift, tm=tm, final_norm=True)

    tm = 512
    zero_conv = jnp.zeros((1, CONV_WIDTH - 1, d), F32)
    zero_ffn = jnp.zeros((1, FFN_CONV_WIDTH - 1, D_FF), F32)
    xp, sc_p, sf0_p = layer0(x_prompt, zero_conv, zero_ffn, shift=1, tm=tm, cb=d, row_chunk=16)
    q, v, kt_p, vt_p, kt_bf = _qkv_t(xp, row2(norm_mix[1]), w_qkv[:, 0:d], w_qkv[:, 2 * d:3 * d],
                                     w_qkv[:, d:2 * d].T, w_qkv[:, 2 * d:3 * d].T, tm=tm)
    att = _prompt_attn(q, kt_bf, v, bias, _suffix_ones(ATT_K_BLOCK))
    xp = _out_proj(att.reshape(bsz * seq, d), xp.reshape(bsz * seq, d), w_o, tm=tm).reshape(bsz, seq, d)
    y_p, sf1_p = ffn1(xp, zero_ffn, shift=1, tm=tm)

    n_tok = dbs * dseq
    to_tm = lambda a: a.transpose(1, 0, 2).reshape(1, a.shape[1] * dbs, a.shape[2])
    from_tm = lambda a: a.reshape(a.shape[1] // dbs, dbs, a.shape[2]).transpose(1, 0, 2)
    xs, sc_s, sf0_s = layer0(to_tm(x_sample), to_tm(state_conv[0]), to_tm(state_ffn[0]),
                             shift=dbs, tm=n_tok, cb=2 * LANES, row_chunk=64)
    q, k_s, v_s = _qkv(xs.reshape(n_tok, d), row2(norm_mix[1]), w_qkv, tm=n_tok)
    q, k_s, v_s = (from_tm(a.reshape(1, n_tok, d)) for a in (q, k_s, v_s))
    pad = ((0, 0), (0, NEW_ROWS - dseq), (0, 0))
    n_pool = cache_k.shape[1]
    pages_t = lambda c: c[0].transpose(0, 2, 3, 1).reshape(n_pool, d, PAGE_SIZE)
    step_keys = PAGES_PER_STEP * PAGE_SIZE
    bias_rows = jnp.broadcast_to(jnp.tile(bias, dseq)[:, None], (dseq * N_HEADS, step_keys))
    u_tot = jnp.concatenate([_suffix_ones(step_keys), jnp.ones((step_keys, LANES), BF16)], axis=1)
    att = _sample_attn(q, jnp.pad(k_s, pad), jnp.pad(v_s, pad), bias_rows, u_tot,
                       pages_t(cache_k), pages_t(cache_v), page_table)
    xs = _out_proj(to_tm(att).reshape(n_tok, d).astype(BF16), xs.reshape(n_tok, d), w_o, tm=n_tok)
    y_s, sf1_s = ffn1(xs.reshape(1, n_tok, d), to_tm(state_ffn[1]), shift=dbs, tm=n_tok)

    heads_t = lambda a: a.reshape(bsz, N_HEADS, HEAD_DIM, seq).transpose(0, 3, 1, 2)[None]
    heads = lambda a: a.reshape(1, dbs, dseq, N_HEADS, HEAD_DIM)
    return (
        y_p,
        from_tm(y_s),
        sc_p[None],
        jnp.stack([sf0_p, sf1_p]),
        heads_t(kt_p),
        heads_t(vt_p),
        from_tm(sc_s)[None],
        jnp.stack([from_tm(sf0_s), from_tm(sf1_s)]),
        heads(k_s),
        heads(v_s),
    )
```

```python
import functools

import jax
import jax.numpy as jnp
from jax import lax
from jax.experimental import pallas as pl
from jax.experimental.pallas import tpu as pltpu

F32 = jnp.float32
BF16 = jnp.bfloat16

D_MODEL = 1024
D_FF = 2816
N_HEADS = 16
HEAD_DIM = 64
CONV_WIDTH = 31
FFN_CONV_WIDTH = 3
PAGE_SIZE = 128
EPS = 1e-6

SUBLANES = 8
LANES = 128
VMEM_BYTES_V7X = 64 * 1024 * 1024
VMEM_LIMIT = VMEM_BYTES_V7X - 8 * 1024 * 1024

HEADS_PER_BLOCK = LANES // HEAD_DIM
ATT_Q_BLOCK = 512
ATT_K_BLOCK = 256
PAGES_PER_STEP = 8
NEW_ROWS = 16
LOG2E = 1.4426950408889634
MASKED_LOG2 = -1e30


def _params(semantics):
    return pltpu.CompilerParams(dimension_semantics=semantics, vmem_limit_bytes=VMEM_LIMIT)


def _resident(shape, index_map):
    return pl.BlockSpec(shape, index_map, pipeline_mode=pl.Buffered(1))


def _rms(x, g):
    return x * lax.rsqrt(jnp.mean(x * x, axis=-1, keepdims=True) + EPS) * g


def _softplus2(z):
    return jnp.maximum(z, 0.0) + jnp.log(1.0 + jnp.exp2(-jnp.abs(z))) * LOG2E


def _silu(x):
    return x * jax.nn.sigmoid(x)


def _dot(a, b):
    return jnp.dot(a, b, preferred_element_type=F32)


def _dot_nt(a, b):
    return lax.dot_general(a, b, (((1,), (1,)), ((), ())), preferred_element_type=F32)


def _glu_conv_kernel(x_ref, g_ref, wa_ref, wg_ref, ba_ref, bg_ref, wdw_ref, bdw_ref, init_ref,
                     y_ref, st_ref, ext_ref, *, shift, tm, n_tiles, row_chunk):
    t = pl.program_id(1)
    c = pl.program_id(2)
    past = (CONV_WIDTH - 1) * shift
    n_copies = ext_ref.shape[0]

    @pl.when(t == 0)
    def _():
        for r in range(n_copies):
            ext_ref[r, c, 0:past - r, :] = init_ref[0, r:past, :]

    h = _rms(x_ref[0], g_ref[...]).astype(BF16)
    a = _dot(h, wa_ref[...]) + ba_ref[...]
    gate = _dot(h, wg_ref[...]) + bg_ref[...]
    glu = a * jax.nn.sigmoid(gate)
    for r in range(n_copies):
        ext_ref[r, c, past - r:past - r + tm, :] = glu

    cb = y_ref.shape[-1]

    def conv_rows(i, carry):
        r0 = pl.multiple_of(i * row_chunk, row_chunk)
        groups = range(row_chunk // SUBLANES)
        accs = [jnp.broadcast_to(bdw_ref[...], (SUBLANES, cb)) for _ in groups]
        for j in range(CONV_WIDTH):
            r = (j * shift) % n_copies
            w = wdw_ref[j * SUBLANES:(j + 1) * SUBLANES, :]
            for gi in groups:
                rows = pl.ds(r0 + (j * shift - r + gi * SUBLANES), SUBLANES)
                accs[gi] = accs[gi] + w * ext_ref[r, c, rows, :]
        for gi in groups:
            y_ref[0, pl.ds(r0 + gi * SUBLANES, SUBLANES), :] = accs[gi]
        return carry

    lax.fori_loop(0, tm // row_chunk, conv_rows, 0)

    @pl.when(t == n_tiles - 1)
    def _():
        st_ref[0] = ext_ref[0, c, tm:tm + past, :]

    if n_tiles > 1:
        for r in range(n_copies):
            ext_ref[r, c, 0:past - r, :] = ext_ref[r, c, tm:tm + past - r, :]


def _glu_conv(x, g, w_in, b_in, w_dw, b_dw, init, *, shift, tm, cb, row_chunk):
    bsz, t_len, d = x.shape
    n_tiles = t_len // tm
    nc = d // cb
    past = (CONV_WIDTH - 1) * shift
    n_copies = 1 if shift % SUBLANES == 0 else SUBLANES
    assert shift == 1 or shift % SUBLANES == 0
    assert t_len % tm == 0 and d % cb == 0 and tm % row_chunk == 0 and row_chunk % SUBLANES == 0
    assert nc == 1 or n_tiles == 1
    assert n_tiles == 1 or past <= tm
    kern = functools.partial(_glu_conv_kernel, shift=shift, tm=tm, n_tiles=n_tiles, row_chunk=row_chunk)
    return pl.pallas_call(
        kern,
        grid=(bsz, n_tiles, nc),
        in_specs=[
            pl.BlockSpec((1, tm, d), lambda b, t, c: (b, t, 0)),
            pl.BlockSpec((1, d), lambda b, t, c: (0, 0)),
            pl.BlockSpec((d, cb), lambda b, t, c: (0, c)),
            pl.BlockSpec((d, cb), lambda b, t, c: (0, nc + c)),
            pl.BlockSpec((1, cb), lambda b, t, c: (0, c)),
            pl.BlockSpec((1, cb), lambda b, t, c: (0, nc + c)),
            pl.BlockSpec((CONV_WIDTH * SUBLANES, cb), lambda b, t, c: (0, c)),
            pl.BlockSpec((1, cb), lambda b, t, c: (0, c)),
            pl.BlockSpec((1, past, cb), lambda b, t, c: (0, 0, c)),
        ],
        out_specs=[
            pl.BlockSpec((1, tm, cb), lambda b, t, c: (b, t, c)),
            pl.BlockSpec((1, past, cb), lambda b, t, c: (b, 0, c)),
        ],
        out_shape=[
            jax.ShapeDtypeStruct((bsz, t_len, d), F32),
            jax.ShapeDtypeStruct((bsz, past, d), F32),
        ],
        scratch_shapes=[pltpu.VMEM((n_copies, nc, past + tm, cb), F32)],
        compiler_params=_params(("arbitrary", "arbitrary", "arbitrary")),
        name="glu_conv",
    )(x, g, w_in, w_in, b_in, b_in, jnp.repeat(w_dw, SUBLANES, axis=0), b_dw, init)


def _ln_out_kernel(y_ref, x_ref, lg_ref, lb_ref, w_ref, b_ref, o_ref):
    y = y_ref[...]
    mu = jnp.mean(y, axis=-1, keepdims=True)
    yc = y - mu
    var = jnp.mean(yc * yc, axis=-1, keepdims=True)
    z = _silu(yc * lax.rsqrt(var + EPS) * lg_ref[...] + lb_ref[...])
    o_ref[...] = x_ref[...] + _dot(z.astype(BF16), w_ref[...]) + b_ref[...]


def _ln_out(y, x, ln_g, ln_b, w_out, b_out, *, tm):
    m, d = x.shape
    return pl.pallas_call(
        _ln_out_kernel,
        grid=(m // tm,),
        in_specs=[
            pl.BlockSpec((tm, d), lambda i: (i, 0)),
            pl.BlockSpec((tm, d), lambda i: (i, 0)),
            pl.BlockSpec((1, d), lambda i: (0, 0)),
            pl.BlockSpec((1, d), lambda i: (0, 0)),
            _resident((d, d), lambda i: (0, 0)),
            pl.BlockSpec((1, d), lambda i: (0, 0)),
        ],
        out_specs=pl.BlockSpec((tm, d), lambda i: (i, 0)),
        out_shape=jax.ShapeDtypeStruct((m, d), F32),
        compiler_params=_params(("arbitrary",)),
        name="ln_out",
    )(y, x, ln_g, ln_b, w_out, b_out)


def _ffn_kernel(x_ref, g_ref, wg_ref, wu_ref, wdw_ref, bdw_ref, wd_ref, init_ref, gf_ref,
                o_ref, st_ref, ext_ref, h_ref, a_ref, *, shift, tm, n_tiles, fc, final_norm):
    t = pl.program_id(1)
    past = (FFN_CONV_WIDTH - 1) * shift
    off = (-past) % SUBLANES
    new = off + past

    @pl.when(t == 0)
    def _():
        ext_ref[off:new, :] = init_ref[0]

    x = x_ref[0]
    h_ref[...] = _rms(x, g_ref[...]).astype(BF16)
    for k in range(D_FF // fc):
        cs = slice(k * fc, (k + 1) * fc)
        ext_ref[new:new + tm, cs] = _dot(h_ref[...], wg_ref[:, cs])
        gc = bdw_ref[:, cs]
        for j in range(FFN_CONV_WIDTH):
            gc = gc + wdw_ref[j:j + 1, cs] * ext_ref[off + j * shift:off + j * shift + tm, cs]
        up = _dot(h_ref[...], wu_ref[:, cs])
        a_ref[:, cs] = (_silu(gc) * up).astype(BF16)
    out = x + _dot(a_ref[...], wd_ref[...])
    if final_norm:
        out = _rms(out, gf_ref[...])
    o_ref[0] = out

    @pl.when(t == n_tiles - 1)
    def _():
        st_ref[0] = ext_ref[off + tm:off + tm + past, :]

    if n_tiles > 1:
        ext_ref[off:new, :] = ext_ref[off + tm:off + tm + past, :]


def _ffn(x, g, w_gate, w_up, w_dw, b_dw, w_down, init, g_final, *, shift, tm, final_norm):
    bsz, t_len, d = x.shape
    f = w_gate.shape[1]
    n_tiles = t_len // tm
    past = (FFN_CONV_WIDTH - 1) * shift
    off = (-past) % SUBLANES
    assert t_len % tm == 0 and (n_tiles == 1 or past <= tm)
    kern = functools.partial(_ffn_kernel, shift=shift, tm=tm, n_tiles=n_tiles, fc=2 * LANES,
                             final_norm=final_norm)
    return pl.pallas_call(
        kern,
        grid=(bsz, n_tiles),
        in_specs=[
            pl.BlockSpec((1, tm, d), lambda b, t: (b, t, 0)),
            pl.BlockSpec((1, d), lambda b, t: (0, 0)),
            _resident((d, f), lambda b, t: (0, 0)),
            _resident((d, f), lambda b, t: (0, 0)),
            pl.BlockSpec((FFN_CONV_WIDTH, f), lambda b, t: (0, 0)),
            pl.BlockSpec((1, f), lambda b, t: (0, 0)),
            _resident((f, d), lambda b, t: (0, 0)),
            pl.BlockSpec((1, past, f), lambda b, t: (0, 0, 0)),
            pl.BlockSpec((1, d), lambda b, t: (0, 0)),
        ],
        out_specs=[
            pl.BlockSpec((1, tm, d), lambda b, t: (b, t, 0)),
            pl.BlockSpec((1, past, f), lambda b, t: (b, 0, 0)),
        ],
        out_shape=[
            jax.ShapeDtypeStruct((bsz, t_len, d), F32),
            jax.ShapeDtypeStruct((bsz, past, f), F32),
        ],
        scratch_shapes=[
            pltpu.VMEM((off + past + tm, f), F32),
            pltpu.VMEM((tm, d), BF16),
            pltpu.VMEM((tm, f), BF16),
        ],
        compiler_params=_params(("arbitrary", "arbitrary")),
        name="conv_ffn",
    )(x, g, w_gate, w_up, w_dw, b_dw, w_down, init, g_final)


Q_SCALE = LOG2E * HEAD_DIM ** -0.5


def _qkv_kernel(x_ref, g_ref, w_ref, q_ref, k_ref, v_ref):
    d = x_ref.shape[-1]
    h = _rms(x_ref[...], g_ref[...]).astype(BF16)
    q_ref[...] = _dot(h, w_ref[:, 0:d]) * Q_SCALE
    k_ref[...] = _dot(h, w_ref[:, d:2 * d])
    v_ref[...] = _dot(h, w_ref[:, 2 * d:3 * d])


def _qkv(x, g, w_qkv, *, tm):
    m, d = x.shape
    row = pl.BlockSpec((tm, d), lambda i: (i, 0))
    return pl.pallas_call(
        _qkv_kernel,
        grid=(m // tm,),
        in_specs=[row, pl.BlockSpec((1, d), lambda i: (0, 0)), _resident((d, 3 * d), lambda i: (0, 0))],
        out_specs=[row, row, row],
        out_shape=[jax.ShapeDtypeStruct((m, d), F32)] * 3,
        compiler_params=_params(("arbitrary",)),
        name="qkv_proj",
    )(x, g, w_qkv)


def _qkv_t_kernel(x_ref, g_ref, wq_ref, wv_ref, wkt_ref, wvt_ref, q_ref, v_ref, kt_ref, vt_ref, ktb_ref):
    h = _rms(x_ref[0], g_ref[...]).astype(BF16)
    q_ref[0] = (_dot(h, wq_ref[...]) * Q_SCALE).astype(BF16)
    v_ref[0] = _dot(h, wv_ref[...]).astype(BF16)
    kt = _dot_nt(wkt_ref[...], h)
    kt_ref[0] = kt
    ktb_ref[0] = kt.astype(BF16)
    vt_ref[0] = _dot_nt(wvt_ref[...], h)


def _qkv_t(x, g, w_q, w_v, w_kt, w_vt, *, tm):
    bsz, t_len, d = x.shape
    row = pl.BlockSpec((1, tm, d), lambda b, t: (b, t, 0))
    col = pl.BlockSpec((1, d, tm), lambda b, t: (b, 0, t))
    weight = lambda: _resident((d, d), lambda b, t: (0, 0))
    return pl.pallas_call(
        _qkv_t_kernel,
        grid=(bsz, t_len // tm),
        in_specs=[row, pl.BlockSpec((1, d), lambda b, t: (0, 0)), weight(), weight(), weight(), weight()],
        out_specs=[row, row, col, col, col],
        out_shape=[
            jax.ShapeDtypeStruct((bsz, t_len, d), BF16),
            jax.ShapeDtypeStruct((bsz, t_len, d), BF16),
            jax.ShapeDtypeStruct((bsz, d, t_len), F32),
            jax.ShapeDtypeStruct((bsz, d, t_len), F32),
            jax.ShapeDtypeStruct((bsz, d, t_len), BF16),
        ],
        compiler_params=_params(("arbitrary", "arbitrary")),
        name="qkv_proj_t",
    )(x, g, w_q, w_v, w_kt, w_vt)


def _prompt_attn_kernel(bias_ref, q_ref, kt_ref, v_ref, u_ref, o_ref):
    hp = pl.program_id(1)
    qi = pl.program_id(2)
    tq = q_ref.shape[1]
    kb = ATT_K_BLOCK
    n_sub = tq // kb

    q = q_ref[0]
    first = lax.broadcasted_iota(jnp.int32, (tq, LANES), 1) < HEAD_DIM
    zero = jnp.zeros_like(q)
    q_heads = (jnp.where(first, q, zero), jnp.where(first, zero, q))
    biases = [bias_ref[hp * HEADS_PER_BLOCK + hh] for hh in range(HEADS_PER_BLOCK)]

    def stage1(base, diagonal):
        pending = []
        for s in reversed(range(n_sub)):
            kt = kt_ref[0, :, pl.ds(pl.multiple_of(base + s * kb, kb), kb)]
            r0 = s * kb if diagonal else 0
            if diagonal:
                row = lax.broadcasted_iota(jnp.int32, (tq - r0, kb), 0)
                col = lax.broadcasted_iota(jnp.int32, (tq - r0, kb), 1)
                visible = col < row
            for hh in range(HEADS_PER_BLOCK):
                z = _dot(q_heads[hh][r0:, :], kt) + biases[hh]
                sp = _softplus2(z)
                if diagonal:
                    sp = jnp.where(visible, sp, 0.0)
                csum = _dot(sp.astype(BF16), u_ref[...])
                d = z - csum
                if diagonal:
                    d = jnp.where(visible, d, MASKED_LOG2)
                pending.append((d, csum[:, 0:1]))
        return tuple(pending)

    def stage2(base, pending, state):
        state = list(state)
        for i, s in enumerate(reversed(range(n_sub))):
            v = v_ref[0, pl.ds(pl.multiple_of(base + s * kb, kb), kb), :]
            for hh in range(HEADS_PER_BLOCK):
                d, total = pending[i * HEADS_PER_BLOCK + hh]
                r0 = tq - d.shape[0]
                run, acc = state[hh]
                pv = _dot(jnp.exp2(d - run[r0:, :]).astype(BF16), v)
                if r0:
                    zeros = jnp.zeros((r0, 1), F32)
                    total = jnp.concatenate([zeros, total], axis=0)
                    pv = jnp.concatenate([jnp.broadcast_to(zeros, (r0, LANES)), pv], axis=0)
                state[hh] = (run + total, acc + pv)
        return tuple(state)

    state = tuple((jnp.zeros((tq, 1), F32), jnp.zeros((tq, LANES), F32)) for _ in range(HEADS_PER_BLOCK))
    state = stage2(qi * tq, stage1(qi * tq, True), state)

    def body(it, state):
        base = (qi - 1 - it) * tq
        return stage2(base, stage1(base, False), state)

    state = lax.fori_loop(0, qi, body, state)
    o_ref[0] = jnp.where(first, state[0][1], state[1][1]).astype(o_ref.dtype)


def _prompt_attn(q, kt, v, bias, u):
    bsz, t_len, d = q.shape
    tq = ATT_Q_BLOCK
    return pl.pallas_call(
        _prompt_attn_kernel,
        grid=(bsz, d // LANES, t_len // tq),
        in_specs=[
            pl.BlockSpec(memory_space=pltpu.SMEM),
            pl.BlockSpec((1, tq, LANES), lambda b, h, i: (b, i, h)),
            pl.BlockSpec((1, LANES, t_len), lambda b, h, i: (b, h, 0)),
            pl.BlockSpec((1, t_len, LANES), lambda b, h, i: (b, 0, h)),
            pl.BlockSpec((ATT_K_BLOCK, ATT_K_BLOCK), lambda b, h, i: (0, 0)),
        ],
        out_specs=pl.BlockSpec((1, tq, LANES), lambda b, h, i: (b, i, h)),
        out_shape=jax.ShapeDtypeStruct((bsz, t_len, d), BF16),
        compiler_params=_params(("arbitrary", "arbitrary", "arbitrary")),
        name="prompt_attn",
    )(bias, q, kt, v, u)


def _sample_attn_kernel(pt_ref, q_ref, kn_ref, vn_ref, bias_ref, u_ref, *rest):
    n = PAGES_PER_STEP
    kt_pages, vt_pages = rest[:n], rest[n:2 * n]
    o_ref, qm_ref, run_ref, acc_ref = rest[2 * n:]
    g = pl.program_id(1)
    n_new = q_ref.shape[1]
    d = q_ref.shape[2]
    rows = n_new * N_HEADS
    step_keys = u_ref.shape[0]

    lane_head = lax.broadcasted_iota(jnp.int32, (N_HEADS, d), 1) // HEAD_DIM
    own_head = lane_head == lax.broadcasted_iota(jnp.int32, (N_HEADS, d), 0)

    def key_block(z, visible, pv):
        keys = z.shape[1]
        z = z + bias_ref[:, 0:keys]
        sp = _softplus2(z)
        if visible is not None:
            sp = jnp.where(visible, sp, 0.0)
        sums = _dot(sp.astype(BF16), u_ref[0:keys, :])
        run = run_ref[...]
        run_keys = run[:, 0:keys] if keys < LANES else jnp.concatenate([run] * (keys // LANES), axis=1)
        w = jnp.exp2(z - sums[:, 0:keys] - run_keys)
        if visible is not None:
            w = jnp.where(visible, w, 0.0)
        acc_ref[...] += pv(w.astype(BF16))
        run_ref[...] = run + sums[:, step_keys:]

    @pl.when(g == 0)
    def _():
        q = q_ref[0]
        qm_ref[...] = jnp.concatenate(
            [jnp.where(own_head, jnp.broadcast_to(q[t:t + 1, :], (N_HEADS, d)), 0.0) for t in range(n_new)],
            axis=0).astype(BF16)
        run_ref[...] = jnp.zeros_like(run_ref)
        acc_ref[...] = jnp.zeros_like(acc_ref)
        key = lax.broadcasted_iota(jnp.int32, (rows, NEW_ROWS), 1)
        qry = lax.broadcasted_iota(jnp.int32, (rows, NEW_ROWS), 0) // N_HEADS
        vn = vn_ref[0].astype(BF16)
        key_block(_dot_nt(qm_ref[...], kn_ref[0].astype(BF16)), key < qry, lambda w: _dot(w, vn))

    kt = jnp.concatenate([p[0].astype(BF16) for p in kt_pages], axis=1)
    vt = jnp.concatenate([p[0].astype(BF16) for p in vt_pages], axis=1)
    key_block(_dot(qm_ref[...], kt), None, lambda w: _dot_nt(w, vt))

    @pl.when(g == pl.num_programs(1) - 1)
    def _():
        for t in range(n_new):
            part = acc_ref[t * N_HEADS:(t + 1) * N_HEADS, :]
            o_ref[0, t:t + 1, :] = jnp.sum(jnp.where(own_head, part, 0.0), axis=0, keepdims=True)


def _sample_attn(q, k_new, v_new, bias_rows, u, cache_kt, cache_vt, page_table):
    bsz, n_new, d = q.shape
    n_pages = page_table.shape[1]
    n = PAGES_PER_STEP
    assert n_pages % n == 0
    rows = n_new * N_HEADS

    def page_map(s):
        return lambda b, g, pt: (pt[b * n_pages + (n_pages - (g + 1) * n + s)], 0, 0)

    page_specs = lambda: [pl.BlockSpec((1, d, PAGE_SIZE), page_map(s)) for s in range(n)]
    grid_spec = pltpu.PrefetchScalarGridSpec(
        num_scalar_prefetch=1,
        grid=(bsz, n_pages // n),
        in_specs=[
            pl.BlockSpec((1, n_new, d), lambda b, g, pt: (b, 0, 0)),
            pl.BlockSpec((1, NEW_ROWS, d), lambda b, g, pt: (b, 0, 0)),
            pl.BlockSpec((1, NEW_ROWS, d), lambda b, g, pt: (b, 0, 0)),
            pl.BlockSpec((rows, n * PAGE_SIZE), lambda b, g, pt: (0, 0)),
            _resident((n * PAGE_SIZE, n * PAGE_SIZE + LANES), lambda b, g, pt: (0, 0)),
        ] + page_specs() + page_specs(),
        out_specs=pl.BlockSpec((1, n_new, d), lambda b, g, pt: (b, 0, 0)),
        scratch_shapes=[
            pltpu.VMEM((rows, d), BF16),
            pltpu.VMEM((rows, LANES), F32),
            pltpu.VMEM((rows, d), F32),
        ],
    )
    return pl.pallas_call(
        _sample_attn_kernel,
        grid_spec=grid_spec,
        out_shape=jax.ShapeDtypeStruct((bsz, n_new, d), F32),
        compiler_params=_params(("arbitrary", "arbitrary")),
        name="sample_attn",
    )(page_table.reshape(-1), q, k_new, v_new, bias_rows, u, *([cache_kt] * n), *([cache_vt] * n))


def _out_proj_kernel(a_ref, x_ref, w_ref, o_ref):
    o_ref[...] = x_ref[...] + _dot(a_ref[...], w_ref[...])


def _out_proj(att, x, w_o, *, tm):
    m, d = x.shape
    row = pl.BlockSpec((tm, d), lambda i: (i, 0))
    return pl.pallas_call(
        _out_proj_kernel,
        grid=(m // tm,),
        in_specs=[row, row, _resident((d, d), lambda i: (0, 0))],
        out_specs=row,
        out_shape=jax.ShapeDtypeStruct((m, d), F32),
        compiler_params=_params(("arbitrary",)),
        name="attn_out_proj",
    )(att, x, w_o)


def _suffix_ones(n):
    return (lax.broadcasted_iota(jnp.int32, (n, n), 0) >= lax.broadcasted_iota(jnp.int32, (n, n), 1)).astype(BF16)


def kernel(x_prompt, x_sample, state_conv, state_ffn, cache_k, cache_v, page_table, norm_mix, norm_ffn, norm_final, conv_w_in, conv_b_in, conv_w_dw, conv_b_dw, conv_ln_g, conv_ln_b, conv_w_out, conv_b_out, attn_w_qkv, attn_w_o, attn_b_logit, ffn_w_gate, ffn_w_up, ffn_w_dw, ffn_b_dw, ffn_w_down):
    d = D_MODEL
    bsz, seq, _ = x_prompt.shape
    dbs, dseq, _ = x_sample.shape
    row2 = lambda a: a.reshape(1, -1)

    w_in = conv_w_in[0].astype(BF16)
    w_out = conv_w_out[0].astype(BF16)
    w_qkv = attn_w_qkv[0].astype(BF16)
    w_o = attn_w_o[0].astype(BF16)
    w_gate = ffn_w_gate.astype(BF16)
    w_up = ffn_w_up.astype(BF16)
    w_down = ffn_w_down.astype(BF16)
    bias = attn_b_logit[0] * LOG2E
    g_final = row2(norm_final)

    def layer0(x, conv_init, ffn_init, *, shift, tm, cb, row_chunk):
        b, t, _ = x.shape
        y, conv_st = _glu_conv(x, row2(norm_mix[0]), w_in, row2(conv_b_in[0]), conv_w_dw[0], row2(conv_b_dw[0]),
                               conv_init, shift=shift, tm=tm, cb=cb, row_chunk=row_chunk)
        x = _ln_out(y.reshape(b * t, d), x.reshape(b * t, d), row2(conv_ln_g[0]), row2(conv_ln_b[0]), w_out,
                    row2(conv_b_out[0]), tm=tm).reshape(b, t, d)
        x, ffn_st = _ffn(x, row2(norm_ffn[0]), w_gate[0], w_up[0], ffn_w_dw[0], row2(ffn_b_dw[0]), w_down[0],
                         ffn_init, g_final, shift=shift, tm=tm, final_norm=False)
        return x, conv_st, ffn_st

    def ffn1(x, ffn_init, *, shift, tm):
        return _ffn(x, row2(norm_ffn[1]), w_gate[1], w_up[1], ffn_w_dw[1], row2(ffn_b_dw[1]), w_down[1],
                    ffn_init, g_final, shift=shift, tm=tm, final_norm=True)

    tm = 512
    zero_conv = jnp.zeros((1, CONV_WIDTH - 1, d), F32)
    zero_ffn = jnp.zeros((1, FFN_CONV_WIDTH - 1, D_FF), F32)
    xp, sc_p, sf0_p = layer0(x_prompt, zero_conv, zero_ffn, shift=1, tm=tm, cb=d, row_chunk=32)
    q, v, kt_p, vt_p, kt_bf = _qkv_t(xp, row2(norm_mix[1]), w_qkv[:, 0:d], w_qkv[:, 2 * d:3 * d],
                                     w_qkv[:, d:2 * d].T, w_qkv[:, 2 * d:3 * d].T, tm=tm)
    att = _prompt_attn(q, kt_bf, v, bias, _suffix_ones(ATT_K_BLOCK))
    xp = _out_proj(att.reshape(bsz * seq, d), xp.reshape(bsz * seq, d), w_o, tm=tm).reshape(bsz, seq, d)
    y_p, sf1_p = ffn1(xp, zero_ffn, shift=1, tm=tm)

    n_tok = dbs * dseq
    to_tm = lambda a: a.transpose(1, 0, 2).reshape(1, a.shape[1] * dbs, a.shape[2])
    from_tm = lambda a: a.reshape(a.shape[1] // dbs, dbs, a.shape[2]).transpose(1, 0, 2)
    xs, sc_s, sf0_s = layer0(to_tm(x_sample), to_tm(state_conv[0]), to_tm(state_ffn[0]),
                             shift=dbs, tm=n_tok, cb=2 * LANES, row_chunk=64)
    q, k_s, v_s = _qkv(xs.reshape(n_tok, d), row2(norm_mix[1]), w_qkv, tm=n_tok)
    q, k_s, v_s = (from_tm(a.reshape(1, n_tok, d)) for a in (q, k_s, v_s))
    pad = ((0, 0), (0, NEW_ROWS - dseq), (0, 0))
    n_pool = cache_k.shape[1]
    pages_t = lambda c: c[0].transpose(0, 2, 3, 1).reshape(n_pool, d, PAGE_SIZE)
    step_keys = PAGES_PER_STEP * PAGE_SIZE
    bias_rows = jnp.broadcast_to(jnp.tile(bias, dseq)[:, None], (dseq * N_HEADS, step_keys))
    u_tot = jnp.concatenate([_suffix_ones(step_keys), jnp.ones((step_keys, LANES), BF16)], axis=1)
    att = _sample_attn(q, jnp.pad(k_s, pad), jnp.pad(v_s, pad), bias_rows, u_tot,
                       pages_t(cache_k), pages_t(cache_v), page_table)
    xs = _out_proj(to_tm(att).reshape(n_tok, d).astype(BF16), xs.reshape(n_tok, d), w_o, tm=n_tok)
    y_s, sf1_s = ffn1(xs.reshape(1, n_tok, d), to_tm(state_ffn[1]), shift=dbs, tm=n_tok)

    heads_t = lambda a: a.reshape(bsz, N_HEADS, HEAD_DIM, seq).transpose(0, 3, 1, 2)[None]
    heads = lambda a: a.reshape(1, dbs, dseq, N_HEADS, HEAD_DIM)
    return (
        y_p,
        from_tm(y_s),
        sc_p[None],
        jnp.stack([sf0_p, sf1_p]),
        heads_t(kt_p),
        heads_t(vt_p),
        from_tm(sc_s)[None],
        jnp.stack([from_tm(sf0_s), from_tm(sf1_s)]),
        heads(k_s),
        heads(v_s),
    )
```

```python
import functools

import jax
import jax.numpy as jnp
from jax import lax
from jax.experimental import pallas as pl
from jax.experimental.pallas import tpu as pltpu

F32 = jnp.float32
BF16 = jnp.bfloat16

D_MODEL = 1024
D_FF = 2816
N_HEADS = 16
HEAD_DIM = 64
CONV_WIDTH = 31
FFN_CONV_WIDTH = 3
PAGE_SIZE = 128
EPS = 1e-6

SUBLANES = 8
LANES = 128
VMEM_BYTES_V7X = 64 * 1024 * 1024
VMEM_LIMIT = VMEM_BYTES_V7X - 8 * 1024 * 1024

HEADS_PER_BLOCK = LANES // HEAD_DIM
ATT_Q_BLOCK = 512
ATT_K_BLOCK = 256
PAGES_PER_STEP = 16
NEW_ROWS = 16
LOG2E = 1.4426950408889634
MASKED_LOG2 = -1e30


def _params(semantics):
    return pltpu.CompilerParams(dimension_semantics=semantics, vmem_limit_bytes=VMEM_LIMIT)


def _resident(shape, index_map):
    return pl.BlockSpec(shape, index_map, pipeline_mode=pl.Buffered(1))


def _rms(x, g):
    return x * lax.rsqrt(jnp.mean(x * x, axis=-1, keepdims=True) + EPS) * g


def _softplus2(z):
    return jnp.maximum(z, 0.0) + jnp.log(1.0 + jnp.exp2(-jnp.abs(z))) * LOG2E


def _silu(x):
    return x * jax.nn.sigmoid(x)


def _dot(a, b):
    return jnp.dot(a, b, preferred_element_type=F32)


def _dot_nt(a, b):
    return lax.dot_general(a, b, (((1,), (1,)), ((), ())), preferred_element_type=F32)


def _glu_conv_kernel(x_ref, g_ref, wa_ref, wg_ref, ba_ref, bg_ref, wdw_ref, bdw_ref, init_ref,
                     y_ref, st_ref, ext_ref, *, shift, tm, n_tiles, row_chunk):
    t = pl.program_id(1)
    c = pl.program_id(2)
    past = (CONV_WIDTH - 1) * shift
    n_copies = ext_ref.shape[0]

    @pl.when(t == 0)
    def _():
        for r in range(n_copies):
            ext_ref[r, c, 0:past - r, :] = init_ref[0, r:past, :]

    h = _rms(x_ref[0], g_ref[...]).astype(BF16)
    a = _dot(h, wa_ref[...]) + ba_ref[...]
    gate = _dot(h, wg_ref[...]) + bg_ref[...]
    glu = a * jax.nn.sigmoid(gate)
    for r in range(n_copies):
        ext_ref[r, c, past - r:past - r + tm, :] = glu

    cb = y_ref.shape[-1]

    def conv_rows(i, carry):
        r0 = pl.multiple_of(i * row_chunk, row_chunk)
        groups = range(row_chunk // SUBLANES)
        accs = [jnp.broadcast_to(bdw_ref[...], (SUBLANES, cb)) for _ in groups]
        for j in range(CONV_WIDTH):
            r = (j * shift) % n_copies
            w = wdw_ref[j * SUBLANES:(j + 1) * SUBLANES, :]
            for gi in groups:
                rows = pl.ds(r0 + (j * shift - r + gi * SUBLANES), SUBLANES)
                accs[gi] = accs[gi] + w * ext_ref[r, c, rows, :]
        for gi in groups:
            y_ref[0, pl.ds(r0 + gi * SUBLANES, SUBLANES), :] = accs[gi]
        return carry

    lax.fori_loop(0, tm // row_chunk, conv_rows, 0)

    @pl.when(t == n_tiles - 1)
    def _():
        st_ref[0] = ext_ref[0, c, tm:tm + past, :]

    if n_tiles > 1:
        for r in range(n_copies):
            ext_ref[r, c, 0:past - r, :] = ext_ref[r, c, tm:tm + past - r, :]


def _glu_conv(x, g, w_in, b_in, w_dw, b_dw, init, *, shift, tm, cb, row_chunk):
    bsz, t_len, d = x.shape
    n_tiles = t_len // tm
    nc = d // cb
    past = (CONV_WIDTH - 1) * shift
    n_copies = 1 if shift % SUBLANES == 0 else SUBLANES
    assert shift == 1 or shift % SUBLANES == 0
    assert t_len % tm == 0 and d % cb == 0 and tm % row_chunk == 0 and row_chunk % SUBLANES == 0
    assert nc == 1 or n_tiles == 1
    assert n_tiles == 1 or past <= tm
    kern = functools.partial(_glu_conv_kernel, shift=shift, tm=tm, n_tiles=n_tiles, row_chunk=row_chunk)
    return pl.pallas_call(
        kern,
        grid=(bsz, n_tiles, nc),
        in_specs=[
            pl.BlockSpec((1, tm, d), lambda b, t, c: (b, t, 0)),
            pl.BlockSpec((1, d), lambda b, t, c: (0, 0)),
            pl.BlockSpec((d, cb), lambda b, t, c: (0, c)),
            pl.BlockSpec((d, cb), lambda b, t, c: (0, nc + c)),
            pl.BlockSpec((1, cb), lambda b, t, c: (0, c)),
            pl.BlockSpec((1, cb), lambda b, t, c: (0, nc + c)),
            pl.BlockSpec((CONV_WIDTH * SUBLANES, cb), lambda b, t, c: (0, c)),
            pl.BlockSpec((1, cb), lambda b, t, c: (0, c)),
            pl.BlockSpec((1, past, cb), lambda b, t, c: (0, 0, c)),
        ],
        out_specs=[
            pl.BlockSpec((1, tm, cb), lambda b, t, c: (b, t, c)),
            pl.BlockSpec((1, past, cb), lambda b, t, c: (b, 0, c)),
        ],
        out_shape=[
            jax.ShapeDtypeStruct((bsz, t_len, d), F32),
            jax.ShapeDtypeStruct((bsz, past, d), F32),
        ],
        scratch_shapes=[pltpu.VMEM((n_copies, nc, past + tm, cb), F32)],
        compiler_params=_params(("arbitrary", "arbitrary", "arbitrary")),
        name="glu_conv",
    )(x, g, w_in, w_in, b_in, b_in, jnp.repeat(w_dw, SUBLANES, axis=0), b_dw, init)


def _ln_out_kernel(y_ref, x_ref, lg_ref, lb_ref, w_ref, b_ref, o_ref):
    y = y_ref[...]
    mu = jnp.mean(y, axis=-1, keepdims=True)
    yc = y - mu
    var = jnp.mean(yc * yc, axis=-1, keepdims=True)
    z = _silu(yc * lax.rsqrt(var + EPS) * lg_ref[...] + lb_ref[...])
    o_ref[...] = x_ref[...] + _dot(z.astype(BF16), w_ref[...]) + b_ref[...]


def _ln_out(y, x, ln_g, ln_b, w_out, b_out, *, tm):
    m, d = x.shape
    return pl.pallas_call(
        _ln_out_kernel,
        grid=(m // tm,),
        in_specs=[
            pl.BlockSpec((tm, d), lambda i: (i, 0)),
            pl.BlockSpec((tm, d), lambda i: (i, 0)),
            pl.BlockSpec((1, d), lambda i: (0, 0)),
            pl.BlockSpec((1, d), lambda i: (0, 0)),
            _resident((d, d), lambda i: (0, 0)),
            pl.BlockSpec((1, d), lambda i: (0, 0)),
        ],
        out_specs=pl.BlockSpec((tm, d), lambda i: (i, 0)),
        out_shape=jax.ShapeDtypeStruct((m, d), F32),
        compiler_params=_params(("arbitrary",)),
        name="ln_out",
    )(y, x, ln_g, ln_b, w_out, b_out)


def _ffn_kernel(x_ref, g_ref, wg_ref, wu_ref, wdw_ref, bdw_ref, wd_ref, init_ref, gf_ref,
                o_ref, st_ref, ext_ref, h_ref, a_ref, *, shift, tm, n_tiles, fc, final_norm):
    t = pl.program_id(1)
    past = (FFN_CONV_WIDTH - 1) * shift
    off = (-past) % SUBLANES
    new = off + past

    @pl.when(t == 0)
    def _():
        ext_ref[off:new, :] = init_ref[0]

    x = x_ref[0]
    h_ref[...] = _rms(x, g_ref[...]).astype(BF16)
    for k in range(D_FF // fc):
        cs = slice(k * fc, (k + 1) * fc)
        ext_ref[new:new + tm, cs] = _dot(h_ref[...], wg_ref[:, cs])
        gc = bdw_ref[:, cs]
        for j in range(FFN_CONV_WIDTH):
            gc = gc + wdw_ref[j:j + 1, cs] * ext_ref[off + j * shift:off + j * shift + tm, cs]
        up = _dot(h_ref[...], wu_ref[:, cs])
        a_ref[:, cs] = (_silu(gc) * up).astype(BF16)
    out = x + _dot(a_ref[...], wd_ref[...])
    if final_norm:
        out = _rms(out, gf_ref[...])
    o_ref[0] = out

    @pl.when(t == n_tiles - 1)
    def _():
        st_ref[0] = ext_ref[off + tm:off + tm + past, :]

    if n_tiles > 1:
        ext_ref[off:new, :] = ext_ref[off + tm:off + tm + past, :]


def _ffn(x, g, w_gate, w_up, w_dw, b_dw, w_down, init, g_final, *, shift, tm, final_norm):
    bsz, t_len, d = x.shape
    f = w_gate.shape[1]
    n_tiles = t_len // tm
    past = (FFN_CONV_WIDTH - 1) * shift
    off = (-past) % SUBLANES
    assert t_len % tm == 0 and (n_tiles == 1 or past <= tm)
    kern = functools.partial(_ffn_kernel, shift=shift, tm=tm, n_tiles=n_tiles, fc=2 * LANES,
                             final_norm=final_norm)
    return pl.pallas_call(
        kern,
        grid=(bsz, n_tiles),
        in_specs=[
            pl.BlockSpec((1, tm, d), lambda b, t: (b, t, 0)),
            pl.BlockSpec((1, d), lambda b, t: (0, 0)),
            _resident((d, f), lambda b, t: (0, 0)),
            _resident((d, f), lambda b, t: (0, 0)),
            pl.BlockSpec((FFN_CONV_WIDTH, f), lambda b, t: (0, 0)),
            pl.BlockSpec((1, f), lambda b, t: (0, 0)),
            _resident((f, d), lambda b, t: (0, 0)),
            pl.BlockSpec((1, past, f), lambda b, t: (0, 0, 0)),
            pl.BlockSpec((1, d), lambda b, t: (0, 0)),
        ],
        out_specs=[
            pl.BlockSpec((1, tm, d), lambda b, t: (b, t, 0)),
            pl.BlockSpec((1, past, f), lambda b, t: (b, 0, 0)),
        ],
        out_shape=[
            jax.ShapeDtypeStruct((bsz, t_len, d), F32),
            jax.ShapeDtypeStruct((bsz, past, f), F32),
        ],
        scratch_shapes=[
            pltpu.VMEM((off + past + tm, f), F32),
            pltpu.VMEM((tm, d), BF16),
            pltpu.VMEM((tm, f), BF16),
        ],
        compiler_params=_params(("arbitrary", "arbitrary")),
        name="conv_ffn",
    )(x, g, w_gate, w_up, w_dw, b_dw, w_down, init, g_final)


Q_SCALE = LOG2E * HEAD_DIM ** -0.5


def _qkv_kernel(x_ref, g_ref, w_ref, q_ref, k_ref, v_ref):
    d = x_ref.shape[-1]
    h = _rms(x_ref[...], g_ref[...]).astype(BF16)
    q_ref[...] = _dot(h, w_ref[:, 0:d]) * Q_SCALE
    k_ref[...] = _dot(h, w_ref[:, d:2 * d])
    v_ref[...] = _dot(h, w_ref[:, 2 * d:3 * d])


def _qkv(x, g, w_qkv, *, tm):
    m, d = x.shape
    row = pl.BlockSpec((tm, d), lambda i: (i, 0))
    return pl.pallas_call(
        _qkv_kernel,
        grid=(m // tm,),
        in_specs=[row, pl.BlockSpec((1, d), lambda i: (0, 0)), _resident((d, 3 * d), lambda i: (0, 0))],
        out_specs=[row, row, row],
        out_shape=[jax.ShapeDtypeStruct((m, d), F32)] * 3,
        compiler_params=_params(("arbitrary",)),
        name="qkv_proj",
    )(x, g, w_qkv)


def _qkv_t_kernel(x_ref, g_ref, wq_ref, wv_ref, wkt_ref, wvt_ref, q_ref, v_ref, kt_ref, vt_ref, ktb_ref):
    h = _rms(x_ref[0], g_ref[...]).astype(BF16)
    q_ref[0] = (_dot(h, wq_ref[...]) * Q_SCALE).astype(BF16)
    v_ref[0] = _dot(h, wv_ref[...]).astype(BF16)
    kt = _dot_nt(wkt_ref[...], h)
    kt_ref[0] = kt
    ktb_ref[0] = kt.astype(BF16)
    vt_ref[0] = _dot_nt(wvt_ref[...], h)


def _qkv_t(x, g, w_q, w_v, w_kt, w_vt, *, tm):
    bsz, t_len, d = x.shape
    row = pl.BlockSpec((1, tm, d), lambda b, t: (b, t, 0))
    col = pl.BlockSpec((1, d, tm), lambda b, t: (b, 0, t))
    weight = lambda: _resident((d, d), lambda b, t: (0, 0))
    return pl.pallas_call(
        _qkv_t_kernel,
        grid=(bsz, t_len // tm),
        in_specs=[row, pl.BlockSpec((1, d), lambda b, t: (0, 0)), weight(), weight(), weight(), weight()],
        out_specs=[row, row, col, col, col],
        out_shape=[
            jax.ShapeDtypeStruct((bsz, t_len, d), BF16),
            jax.ShapeDtypeStruct((bsz, t_len, d), BF16),
            jax.ShapeDtypeStruct((bsz, d, t_len), F32),
            jax.ShapeDtypeStruct((bsz, d, t_len), F32),
            jax.ShapeDtypeStruct((bsz, d, t_len), BF16),
        ],
        compiler_params=_params(("arbitrary", "arbitrary")),
        name="qkv_proj_t",
    )(x, g, w_q, w_v, w_kt, w_vt)


def _prompt_attn_kernel(bias_ref, q_ref, kt_ref, v_ref, u_ref, o_ref):
    hp = pl.program_id(1)
    qi = pl.program_id(2)
    tq = q_ref.shape[1]
    kb = ATT_K_BLOCK
    n_sub = tq // kb

    q = q_ref[0]
    first = lax.broadcasted_iota(jnp.int32, (tq, LANES), 1) < HEAD_DIM
    zero = jnp.zeros_like(q)
    q_heads = (jnp.where(first, q, zero), jnp.where(first, zero, q))
    biases = [bias_ref[hp * HEADS_PER_BLOCK + hh] for hh in range(HEADS_PER_BLOCK)]

    def stage1(base, diagonal):
        pending = []
        for s in reversed(range(n_sub)):
            kt = kt_ref[0, :, pl.ds(pl.multiple_of(base + s * kb, kb), kb)]
            r0 = s * kb if diagonal else 0
            if diagonal:
                row = lax.broadcasted_iota(jnp.int32, (tq - r0, kb), 0)
                col = lax.broadcasted_iota(jnp.int32, (tq - r0, kb), 1)
                visible = col < row
            for hh in range(HEADS_PER_BLOCK):
                z = _dot(q_heads[hh][r0:, :], kt) + biases[hh]
                sp = _softplus2(z)
                if diagonal:
                    sp = jnp.where(visible, sp, 0.0)
                csum = _dot(sp.astype(BF16), u_ref[...])
                d = z - csum
                if diagonal:
                    d = jnp.where(visible, d, MASKED_LOG2)
                pending.append((d, csum[:, 0:1]))
        return tuple(pending)

    def stage2(base, pending, state):
        state = list(state)
        for i, s in enumerate(reversed(range(n_sub))):
            v = v_ref[0, pl.ds(pl.multiple_of(base + s * kb, kb), kb), :]
            for hh in range(HEADS_PER_BLOCK):
                d, total = pending[i * HEADS_PER_BLOCK + hh]
                r0 = tq - d.shape[0]
                run, acc = state[hh]
                pv = _dot(jnp.exp2(d - run[r0:, :]).astype(BF16), v)
                if r0:
                    zeros = jnp.zeros((r0, 1), F32)
                    total = jnp.concatenate([zeros, total], axis=0)
                    pv = jnp.concatenate([jnp.broadcast_to(zeros, (r0, LANES)), pv], axis=0)
                state[hh] = (run + total, acc + pv)
        return tuple(state)

    state = tuple((jnp.zeros((tq, 1), F32), jnp.zeros((tq, LANES), F32)) for _ in range(HEADS_PER_BLOCK))
    state = stage2(qi * tq, stage1(qi * tq, True), state)

    odd = qi % 2

    def single(state):
        base = (qi - 1) * tq
        return stage2(base, stage1(base, False), state)

    state = lax.cond(odd == 1, single, lambda state: state, state)

    def body(it, state):
        base = (qi - 1 - odd - 2 * it) * tq
        p1 = stage1(base, False)
        p2 = stage1(base - tq, False)
        return stage2(base - tq, p2, stage2(base, p1, state))

    state = lax.fori_loop(0, qi // 2, body, state)
    o_ref[0] = jnp.where(first, state[0][1], state[1][1]).astype(o_ref.dtype)


def _prompt_attn(q, kt, v, bias, u):
    bsz, t_len, d = q.shape
    tq = ATT_Q_BLOCK
    return pl.pallas_call(
        _prompt_attn_kernel,
        grid=(bsz, d // LANES, t_len // tq),
        in_specs=[
            pl.BlockSpec(memory_space=pltpu.SMEM),
            pl.BlockSpec((1, tq, LANES), lambda b, h, i: (b, i, h)),
            pl.BlockSpec((1, LANES, t_len), lambda b, h, i: (b, h, 0)),
            pl.BlockSpec((1, t_len, LANES), lambda b, h, i: (b, 0, h)),
            pl.BlockSpec((ATT_K_BLOCK, ATT_K_BLOCK), lambda b, h, i: (0, 0)),
        ],
        out_specs=pl.BlockSpec((1, tq, LANES), lambda b, h, i: (b, i, h)),
        out_shape=jax.ShapeDtypeStruct((bsz, t_len, d), BF16),
        compiler_params=_params(("arbitrary", "arbitrary", "arbitrary")),
        name="prompt_attn",
    )(bias, q, kt, v, u)


def _sample_attn_kernel(pt_ref, q_ref, kn_ref, vn_ref, bias_ref, u_ref, *rest):
    n = PAGES_PER_STEP
    kt_pages, vt_pages = rest[:n], rest[n:2 * n]
    o_ref, qm_ref, run_ref, acc_ref = rest[2 * n:]
    g = pl.program_id(1)
    n_new = q_ref.shape[1]
    d = q_ref.shape[2]
    rows = n_new * N_HEADS
    step_keys = u_ref.shape[0]

    lane_head = lax.broadcasted_iota(jnp.int32, (N_HEADS, d), 1) // HEAD_DIM
    own_head = lane_head == lax.broadcasted_iota(jnp.int32, (N_HEADS, d), 0)

    def key_block(z, visible, pv):
        keys = z.shape[1]
        z = z + bias_ref[:, 0:keys]
        sp = _softplus2(z)
        if visible is not None:
            sp = jnp.where(visible, sp, 0.0)
        sums = _dot(sp.astype(BF16), u_ref[0:keys, :])
        run = run_ref[...]
        run_keys = run[:, 0:keys] if keys < LANES else jnp.concatenate([run] * (keys // LANES), axis=1)
        w = jnp.exp2(z - sums[:, 0:keys] - run_keys)
        if visible is not None:
            w = jnp.where(visible, w, 0.0)
        acc_ref[...] += pv(w.astype(BF16))
        run_ref[...] = run + sums[:, step_keys:]

    @pl.when(g == 0)
    def _():
        q = q_ref[0]
        qm_ref[...] = jnp.concatenate(
            [jnp.where(own_head, jnp.broadcast_to(q[t:t + 1, :], (N_HEADS, d)), 0.0) for t in range(n_new)],
            axis=0).astype(BF16)
        run_ref[...] = jnp.zeros_like(run_ref)
        acc_ref[...] = jnp.zeros_like(acc_ref)
        key = lax.broadcasted_iota(jnp.int32, (rows, NEW_ROWS), 1)
        qry = lax.broadcasted_iota(jnp.int32, (rows, NEW_ROWS), 0) // N_HEADS
        vn = vn_ref[0].astype(BF16)
        key_block(_dot_nt(qm_ref[...], kn_ref[0].astype(BF16)), key < qry, lambda w: _dot(w, vn))

    kt = jnp.concatenate([p[0].astype(BF16) for p in kt_pages], axis=1)
    vt = jnp.concatenate([p[0].astype(BF16) for p in vt_pages], axis=1)
    key_block(_dot(qm_ref[...], kt), None, lambda w: _dot_nt(w, vt))

    @pl.when(g == pl.num_programs(1) - 1)
    def _():
        for t in range(n_new):
            part = acc_ref[t * N_HEADS:(t + 1) * N_HEADS, :]
            o_ref[0, t:t + 1, :] = jnp.sum(jnp.where(own_head, part, 0.0), axis=0, keepdims=True)


def _sample_attn(q, k_new, v_new, bias_rows, u, cache_kt, cache_vt, page_table):
    bsz, n_new, d = q.shape
    n_pages = page_table.shape[1]
    n = PAGES_PER_STEP
    assert n_pages % n == 0
    rows = n_new * N_HEADS

    def page_map(s):
        return lambda b, g, pt: (pt[b * n_pages + (n_pages - (g + 1) * n + s)], 0, 0)

    page_specs = lambda: [pl.BlockSpec((1, d, PAGE_SIZE), page_map(s)) for s in range(n)]
    grid_spec = pltpu.PrefetchScalarGridSpec(
        num_scalar_prefetch=1,
        grid=(bsz, n_pages // n),
        in_specs=[
            pl.BlockSpec((1, n_new, d), lambda b, g, pt: (b, 0, 0)),
            pl.BlockSpec((1, NEW_ROWS, d), lambda b, g, pt: (b, 0, 0)),
            pl.BlockSpec((1, NEW_ROWS, d), lambda b, g, pt: (b, 0, 0)),
            pl.BlockSpec((rows, n * PAGE_SIZE), lambda b, g, pt: (0, 0)),
            _resident((n * PAGE_SIZE, n * PAGE_SIZE + LANES), lambda b, g, pt: (0, 0)),
        ] + page_specs() + page_specs(),
        out_specs=pl.BlockSpec((1, n_new, d), lambda b, g, pt: (b, 0, 0)),
        scratch_shapes=[
            pltpu.VMEM((rows, d), BF16),
            pltpu.VMEM((rows, LANES), F32),
            pltpu.VMEM((rows, d), F32),
        ],
    )
    return pl.pallas_call(
        _sample_attn_kernel,
        grid_spec=grid_spec,
        out_shape=jax.ShapeDtypeStruct((bsz, n_new, d), F32),
        compiler_params=_params(("arbitrary", "arbitrary")),
        name="sample_attn",
    )(page_table.reshape(-1), q, k_new, v_new, bias_rows, u, *([cache_kt] * n), *([cache_vt] * n))


def _out_proj_kernel(a_ref, x_ref, w_ref, o_ref):
    o_ref[...] = x_ref[...] + _dot(a_ref[...], w_ref[...])


def _out_proj(att, x, w_o, *, tm):
    m, d = x.shape
    row = pl.BlockSpec((tm, d), lambda i: (i, 0))
    return pl.pallas_call(
        _out_proj_kernel,
        grid=(m // tm,),
        in_specs=[row, row, _resident((d, d), lambda i: (0, 0))],
        out_specs=row,
        out_shape=jax.ShapeDtypeStruct((m, d), F32),
        compiler_params=_params(("arbitrary",)),
        name="attn_out_proj",
    )(att, x, w_o)


def _suffix_ones(n):
    return (lax.broadcasted_iota(jnp.int32, (n, n), 0) >= lax.broadcasted_iota(jnp.int32, (n, n), 1)).astype(BF16)


def kernel(x_prompt, x_sample, state_conv, state_ffn, cache_k, cache_v, page_table, norm_mix, norm_ffn, norm_final, conv_w_in, conv_b_in, conv_w_dw, conv_b_dw, conv_ln_g, conv_ln_b, conv_w_out, conv_b_out, attn_w_qkv, attn_w_o, attn_b_logit, ffn_w_gate, ffn_w_up, ffn_w_dw, ffn_b_dw, ffn_w_down):
    d = D_MODEL
    bsz, seq, _ = x_prompt.shape
    dbs, dseq, _ = x_sample.shape
    row2 = lambda a: a.reshape(1, -1)

    w_in = conv_w_in[0].astype(BF16)
    w_out = conv_w_out[0].astype(BF16)
    w_qkv = attn_w_qkv[0].astype(BF16)
    w_o = attn_w_o[0].astype(BF16)
    w_gate = ffn_w_gate.astype(BF16)
    w_up = ffn_w_up.astype(BF16)
    w_down = ffn_w_down.astype(BF16)
    bias = attn_b_logit[0] * LOG2E
    g_final = row2(norm_final)

    def layer0(x, conv_init, ffn_init, *, shift, tm, cb, row_chunk):
        b, t, _ = x.shape
        y, conv_st = _glu_conv(x, row2(norm_mix[0]), w_in, row2(conv_b_in[0]), conv_w_dw[0], row2(conv_b_dw[0]),
                               conv_init, shift=shift, tm=tm, cb=cb, row_chunk=row_chunk)
        x = _ln_out(y.reshape(b * t, d), x.reshape(b * t, d), row2(conv_ln_g[0]), row2(conv_ln_b[0]), w_out,
                    row2(conv_b_out[0]), tm=tm).reshape(b, t, d)
        x, ffn_st = _ffn(x, row2(norm_ffn[0]), w_gate[0], w_up[0], ffn_w_dw[0], row2(ffn_b_dw[0]), w_down[0],
                         ffn_init, g_final, shift=shift, tm=tm, final_norm=False)
        return x, conv_st, ffn_st

    def ffn1(x, ffn_init, *, shift, tm):
        return _ffn(x, row2(norm_ffn[1]), w_gate[1], w_up[1], ffn_w_dw[1], row2(ffn_b_dw[1]), w_down[1],
                    ffn_init, g_final, shift=shift, tm=tm, final_norm=True)

    tm = 512
    zero_conv = jnp.zeros((1, CONV_WIDTH - 1, d), F32)
    zero_ffn = jnp.zeros((1, FFN_CONV_WIDTH - 1, D_FF), F32)
    xp, sc_p, sf0_p = layer0(x_prompt, zero_conv, zero_ffn, shift=1, tm=tm, cb=d, row_chunk=32)
    q, v, kt_p, vt_p, kt_bf = _qkv_t(xp, row2(norm_mix[1]), w_qkv[:, 0:d], w_qkv[:, 2 * d:3 * d],
                                     w_qkv[:, d:2 * d].T, w_qkv[:, 2 * d:3 * d].T, tm=tm)
    att = _prompt_attn(q, kt_bf, v, bias, _suffix_ones(ATT_K_BLOCK))
    xp = _out_proj(att.reshape(bsz * seq, d), xp.reshape(bsz * seq, d), w_o, tm=tm).reshape(bsz, seq, d)
    y_p, sf1_p = ffn1(xp, zero_ffn, shift=1, tm=tm)

    n_tok = dbs * dseq
    to_tm = lambda a: a.transpose(1, 0, 2).reshape(1, a.shape[1] * dbs, a.shape[2])
    from_tm = lambda a: a.reshape(a.shape[1] // dbs, dbs, a.shape[2]).transpose(1, 0, 2)
    xs, sc_s, sf0_s = layer0(to_tm(x_sample), to_tm(state_conv[0]), to_tm(state_ffn[0]),
                             shift=dbs, tm=n_tok, cb=2 * LANES, row_chunk=64)
    q, k_s, v_s = _qkv(xs.reshape(n_tok, d), row2(norm_mix[1]), w_qkv, tm=n_tok)
    q, k_s, v_s = (from_tm(a.reshape(1, n_tok, d)) for a in (q, k_s, v_s))
    pad = ((0, 0), (0, NEW_ROWS - dseq), (0, 0))
    n_pool = cache_k.shape[1]
    pages_t = lambda c: c[0].transpose(0, 2, 3, 1).reshape(n_pool, d, PAGE_SIZE)
    step_keys = PAGES_PER_STEP * PAGE_SIZE
    bias_rows = jnp.broadcast_to(jnp.tile(bias, dseq)[:, None], (dseq * N_HEADS, step_keys))
    u_tot = jnp.concatenate([_suffix_ones(step_keys), jnp.ones((step_keys, LANES), BF16)], axis=1)
    att = _sample_attn(q, jnp.pad(k_s, pad), jnp.pad(v_s, pad), bias_rows, u_tot,
                       pages_t(cache_k), pages_t(cache_v), page_table)
    xs = _out_proj(to_tm(att).reshape(n_tok, d).astype(BF16), xs.reshape(n_tok, d), w_o, tm=n_tok)
    y_s, sf1_s = ffn1(xs.reshape(1, n_tok, d), to_tm(state_ffn[1]), shift=dbs, tm=n_tok)

    heads_t = lambda a: a.reshape(bsz, N_HEADS, HEAD_DIM, seq).transpose(0, 3, 1, 2)[None]
    heads = lambda a: a.reshape(1, dbs, dseq, N_HEADS, HEAD_DIM)
    return (
        y_p,
        from_tm(y_s),
        sc_p[None],
        jnp.stack([sf0_p, sf1_p]),
        heads_t(kt_p),
        heads_t(vt_p),
        from_tm(sc_s)[None],
        jnp.stack([from_tm(sf0_s), from_tm(sf1_s)]),
        heads(k_s),
        heads(v_s),
    )
```

```python
import functools

import jax
import jax.numpy as jnp
from jax import lax
from jax.experimental import pallas as pl
from jax.experimental.pallas import tpu as pltpu

F32 = jnp.float32
BF16 = jnp.bfloat16

D_MODEL = 1024
D_FF = 2816
N_HEADS = 16
HEAD_DIM = 64
CONV_WIDTH = 31
FFN_CONV_WIDTH = 3
PAGE_SIZE = 128
EPS = 1e-6

SUBLANES = 8
LANES = 128
VMEM_BYTES_V7X = 64 * 1024 * 1024
VMEM_LIMIT = VMEM_BYTES_V7X - 8 * 1024 * 1024

HEADS_PER_BLOCK = LANES // HEAD_DIM
ATT_Q_BLOCK = 512
ATT_K_BLOCK = 256
PAGES_PER_STEP = 16
NEW_ROWS = 16
LOG2E = 1.4426950408889634
MASKED_LOG2 = -1e30


def _params(semantics):
    return pltpu.CompilerParams(dimension_semantics=semantics, vmem_limit_bytes=VMEM_LIMIT)


def _resident(shape, index_map):
    return pl.BlockSpec(shape, index_map, pipeline_mode=pl.Buffered(1))


def _rms(x, g):
    return x * lax.rsqrt(jnp.mean(x * x, axis=-1, keepdims=True) + EPS) * g


def _softplus2(z):
    return jnp.maximum(z, 0.0) + jnp.log(1.0 + jnp.exp2(-jnp.abs(z))) * LOG2E


def _silu(x):
    return x * jax.nn.sigmoid(x)


def _dot(a, b):
    return jnp.dot(a, b, preferred_element_type=F32)


def _dot_nt(a, b):
    return lax.dot_general(a, b, (((1,), (1,)), ((), ())), preferred_element_type=F32)


def _ln_silu(y, ln_g, ln_b):
    mu = jnp.mean(y, axis=-1, keepdims=True)
    yc = y - mu
    var = jnp.mean(yc * yc, axis=-1, keepdims=True)
    return _silu(yc * lax.rsqrt(var + EPS) * ln_g + ln_b)


def _glu_conv_kernel(x_ref, g_ref, wa_ref, wg_ref, ba_ref, bg_ref, wdw_ref, bdw_ref, init_ref, *rest,
                     shift, tm, n_tiles, row_chunk, fused):
    if fused:
        lg_ref, lb_ref, wo_ref, bo_ref, o_ref, st_ref, ext_ref, y_ref = rest
    else:
        y_ref, st_ref, ext_ref = rest
    t = pl.program_id(1)
    c = pl.program_id(2)
    past = (CONV_WIDTH - 1) * shift
    n_copies = ext_ref.shape[0]

    @pl.when(t == 0)
    def _():
        for r in range(n_copies):
            ext_ref[r, c, 0:past - r, :] = init_ref[0, r:past, :]

    h = _rms(x_ref[0], g_ref[...]).astype(BF16)
    a = _dot(h, wa_ref[...]) + ba_ref[...]
    gate = _dot(h, wg_ref[...]) + bg_ref[...]
    glu = a * jax.nn.sigmoid(gate)
    for r in range(n_copies):
        ext_ref[r, c, past - r:past - r + tm, :] = glu

    cb = y_ref.shape[-1]

    def conv_rows(i, carry):
        r0 = pl.multiple_of(i * row_chunk, row_chunk)
        groups = range(row_chunk // SUBLANES)
        accs = [jnp.broadcast_to(bdw_ref[...], (SUBLANES, cb)) for _ in groups]
        for j in range(CONV_WIDTH):
            r = (j * shift) % n_copies
            w = wdw_ref[j * SUBLANES:(j + 1) * SUBLANES, :]
            for gi in groups:
                rows = pl.ds(r0 + (j * shift - r + gi * SUBLANES), SUBLANES)
                accs[gi] = accs[gi] + w * ext_ref[r, c, rows, :]
        for gi in groups:
            y_ref[0, pl.ds(r0 + gi * SUBLANES, SUBLANES), :] = accs[gi]
        return carry

    lax.fori_loop(0, tm // row_chunk, conv_rows, 0)

    if fused:
        z = _ln_silu(y_ref[0], lg_ref[...], lb_ref[...])
        o_ref[0] = x_ref[0] + _dot(z.astype(BF16), wo_ref[...]) + bo_ref[...]

    @pl.when(t == n_tiles - 1)
    def _():
        st_ref[0] = ext_ref[0, c, tm:tm + past, :]

    if n_tiles > 1:
        for r in range(n_copies):
            ext_ref[r, c, 0:past - r, :] = ext_ref[r, c, tm:tm + past - r, :]


def _glu_conv(x, g, w_in, b_in, w_dw, b_dw, init, out_proj, *, shift, tm, cb, row_chunk):
    bsz, t_len, d = x.shape
    n_tiles = t_len // tm
    nc = d // cb
    past = (CONV_WIDTH - 1) * shift
    n_copies = 1 if shift % SUBLANES == 0 else SUBLANES
    assert shift == 1 or shift % SUBLANES == 0
    assert t_len % tm == 0 and d % cb == 0 and tm % row_chunk == 0 and row_chunk % SUBLANES == 0
    assert nc == 1 or n_tiles == 1
    assert n_tiles == 1 or past <= tm
    fused = nc == 1
    kern = functools.partial(_glu_conv_kernel, shift=shift, tm=tm, n_tiles=n_tiles, row_chunk=row_chunk,
                             fused=fused)
    vec = pl.BlockSpec((1, d), lambda b, t, c: (0, 0))
    fused_specs = [vec, vec, _resident((d, d), lambda b, t, c: (0, 0)), vec] if fused else []
    return pl.pallas_call(
        kern,
        grid=(bsz, n_tiles, nc),
        in_specs=[
            pl.BlockSpec((1, tm, d), lambda b, t, c: (b, t, 0)),
            pl.BlockSpec((1, d), lambda b, t, c: (0, 0)),
            pl.BlockSpec((d, cb), lambda b, t, c: (0, c)),
            pl.BlockSpec((d, cb), lambda b, t, c: (0, nc + c)),
            pl.BlockSpec((1, cb), lambda b, t, c: (0, c)),
            pl.BlockSpec((1, cb), lambda b, t, c: (0, nc + c)),
            pl.BlockSpec((CONV_WIDTH * SUBLANES, cb), lambda b, t, c: (0, c)),
            pl.BlockSpec((1, cb), lambda b, t, c: (0, c)),
            pl.BlockSpec((1, past, cb), lambda b, t, c: (0, 0, c)),
        ] + fused_specs,
        out_specs=[
            pl.BlockSpec((1, tm, cb), lambda b, t, c: (b, t, c)),
            pl.BlockSpec((1, past, cb), lambda b, t, c: (b, 0, c)),
        ],
        out_shape=[
            jax.ShapeDtypeStruct((bsz, t_len, d), F32),
            jax.ShapeDtypeStruct((bsz, past, d), F32),
        ],
        scratch_shapes=[pltpu.VMEM((n_copies, nc, past + tm, cb), F32)]
        + ([pltpu.VMEM((1, tm, cb), F32)] if fused else []),
        compiler_params=_params(("arbitrary", "arbitrary", "arbitrary")),
        name="glu_conv",
    )(x, g, w_in, w_in, b_in, b_in, jnp.repeat(w_dw, SUBLANES, axis=0), b_dw, init, *(out_proj if fused else ()))


def _ln_out_kernel(y_ref, x_ref, lg_ref, lb_ref, w_ref, b_ref, o_ref):
    z = _ln_silu(y_ref[...], lg_ref[...], lb_ref[...])
    o_ref[...] = x_ref[...] + _dot(z.astype(BF16), w_ref[...]) + b_ref[...]


def _ln_out(y, x, ln_g, ln_b, w_out, b_out, *, tm):
    m, d = x.shape
    return pl.pallas_call(
        _ln_out_kernel,
        grid=(m // tm,),
        in_specs=[
            pl.BlockSpec((tm, d), lambda i: (i, 0)),
            pl.BlockSpec((tm, d), lambda i: (i, 0)),
            pl.BlockSpec((1, d), lambda i: (0, 0)),
            pl.BlockSpec((1, d), lambda i: (0, 0)),
            _resident((d, d), lambda i: (0, 0)),
            pl.BlockSpec((1, d), lambda i: (0, 0)),
        ],
        out_specs=pl.BlockSpec((tm, d), lambda i: (i, 0)),
        out_shape=jax.ShapeDtypeStruct((m, d), F32),
        compiler_params=_params(("arbitrary",)),
        name="ln_out",
    )(y, x, ln_g, ln_b, w_out, b_out)


def _ffn_kernel(x_ref, g_ref, wg_ref, wu_ref, wdw_ref, bdw_ref, wd_ref, init_ref, gf_ref, *rest,
                shift, tm, n_tiles, fc, final_norm, with_attn):
    if with_attn:
        att_ref, wo_ref, o_ref, st_ref, ext_ref, h_ref, a_ref = rest
    else:
        o_ref, st_ref, ext_ref, h_ref, a_ref = rest
    t = pl.program_id(1)
    past = (FFN_CONV_WIDTH - 1) * shift
    off = (-past) % SUBLANES
    new = off + past

    @pl.when(t == 0)
    def _():
        ext_ref[off:new, :] = init_ref[0]

    x = x_ref[0]
    if with_attn:
        x = x + _dot(att_ref[0], wo_ref[...])
    h_ref[...] = _rms(x, g_ref[...]).astype(BF16)
    for k in range(D_FF // fc):
        cs = slice(k * fc, (k + 1) * fc)
        ext_ref[new:new + tm, cs] = _dot(h_ref[...], wg_ref[:, cs])
        gc = bdw_ref[:, cs]
        for j in range(FFN_CONV_WIDTH):
            gc = gc + wdw_ref[j:j + 1, cs] * ext_ref[off + j * shift:off + j * shift + tm, cs]
        up = _dot(h_ref[...], wu_ref[:, cs])
        a_ref[:, cs] = (_silu(gc) * up).astype(BF16)
    out = x + _dot(a_ref[...], wd_ref[...])
    if final_norm:
        out = _rms(out, gf_ref[...])
    o_ref[0] = out

    @pl.when(t == n_tiles - 1)
    def _():
        st_ref[0] = ext_ref[off + tm:off + tm + past, :]

    if n_tiles > 1:
        ext_ref[off:new, :] = ext_ref[off + tm:off + tm + past, :]


def _ffn(x, g, w_gate, w_up, w_dw, b_dw, w_down, init, g_final, attn, *, shift, tm, final_norm):
    bsz, t_len, d = x.shape
    f = w_gate.shape[1]
    n_tiles = t_len // tm
    past = (FFN_CONV_WIDTH - 1) * shift
    off = (-past) % SUBLANES
    assert t_len % tm == 0 and (n_tiles == 1 or past <= tm)
    kern = functools.partial(_ffn_kernel, shift=shift, tm=tm, n_tiles=n_tiles, fc=2 * LANES,
                             final_norm=final_norm, with_attn=attn is not None)
    attn_specs = [] if attn is None else [pl.BlockSpec((1, tm, d), lambda b, t: (b, t, 0)),
                                          _resident((d, d), lambda b, t: (0, 0))]
    return pl.pallas_call(
        kern,
        grid=(bsz, n_tiles),
        in_specs=[
            pl.BlockSpec((1, tm, d), lambda b, t: (b, t, 0)),
            pl.BlockSpec((1, d), lambda b, t: (0, 0)),
            _resident((d, f), lambda b, t: (0, 0)),
            _resident((d, f), lambda b, t: (0, 0)),
            pl.BlockSpec((FFN_CONV_WIDTH, f), lambda b, t: (0, 0)),
            pl.BlockSpec((1, f), lambda b, t: (0, 0)),
            _resident((f, d), lambda b, t: (0, 0)),
            pl.BlockSpec((1, past, f), lambda b, t: (0, 0, 0)),
            pl.BlockSpec((1, d), lambda b, t: (0, 0)),
        ] + attn_specs,
        out_specs=[
            pl.BlockSpec((1, tm, d), lambda b, t: (b, t, 0)),
            pl.BlockSpec((1, past, f), lambda b, t: (b, 0, 0)),
        ],
        out_shape=[
            jax.ShapeDtypeStruct((bsz, t_len, d), F32),
            jax.ShapeDtypeStruct((bsz, past, f), F32),
        ],
        scratch_shapes=[
            pltpu.VMEM((off + past + tm, f), F32),
            pltpu.VMEM((tm, d), BF16),
            pltpu.VMEM((tm, f), BF16),
        ],
        compiler_params=_params(("arbitrary", "arbitrary")),
        name="conv_ffn",
    )(x, g, w_gate, w_up, w_dw, b_dw, w_down, init, g_final, *(attn or ()))


Q_SCALE = LOG2E * HEAD_DIM ** -0.5


def _qkv_kernel(x_ref, g_ref, w_ref, q_ref, k_ref, v_ref):
    d = x_ref.shape[-1]
    h = _rms(x_ref[...], g_ref[...]).astype(BF16)
    q_ref[...] = _dot(h, w_ref[:, 0:d]) * Q_SCALE
    k_ref[...] = _dot(h, w_ref[:, d:2 * d])
    v_ref[...] = _dot(h, w_ref[:, 2 * d:3 * d])


def _qkv(x, g, w_qkv, *, tm):
    m, d = x.shape
    row = pl.BlockSpec((tm, d), lambda i: (i, 0))
    return pl.pallas_call(
        _qkv_kernel,
        grid=(m // tm,),
        in_specs=[row, pl.BlockSpec((1, d), lambda i: (0, 0)), _resident((d, 3 * d), lambda i: (0, 0))],
        out_specs=[row, row, row],
        out_shape=[jax.ShapeDtypeStruct((m, d), F32)] * 3,
        compiler_params=_params(("arbitrary",)),
        name="qkv_proj",
    )(x, g, w_qkv)


def _qkv_t_kernel(x_ref, g_ref, wq_ref, wv_ref, wkt_ref, wvt_ref, q_ref, v_ref, kt_ref, vt_ref, ktb_ref):
    h = _rms(x_ref[0], g_ref[...]).astype(BF16)
    q_ref[0] = (_dot(h, wq_ref[...]) * Q_SCALE).astype(BF16)
    v_ref[0] = _dot(h, wv_ref[...]).astype(BF16)
    kt = _dot_nt(wkt_ref[...], h)
    kt_ref[0] = kt
    ktb_ref[0] = kt.astype(BF16)
    vt_ref[0] = _dot_nt(wvt_ref[...], h)


def _qkv_t(x, g, w_q, w_v, w_kt, w_vt, *, tm):
    bsz, t_len, d = x.shape
    row = pl.BlockSpec((1, tm, d), lambda b, t: (b, t, 0))
    col = pl.BlockSpec((1, d, tm), lambda b, t: (b, 0, t))
    weight = lambda: _resident((d, d), lambda b, t: (0, 0))
    return pl.pallas_call(
        _qkv_t_kernel,
        grid=(bsz, t_len // tm),
        in_specs=[row, pl.BlockSpec((1, d), lambda b, t: (0, 0)), weight(), weight(), weight(), weight()],
        out_specs=[row, row, col, col, col],
        out_shape=[
            jax.ShapeDtypeStruct((bsz, t_len, d), BF16),
            jax.ShapeDtypeStruct((bsz, t_len, d), BF16),
            jax.ShapeDtypeStruct((bsz, d, t_len), F32),
            jax.ShapeDtypeStruct((bsz, d, t_len), F32),
            jax.ShapeDtypeStruct((bsz, d, t_len), BF16),
        ],
        compiler_params=_params(("arbitrary", "arbitrary")),
        name="qkv_proj_t",
    )(x, g, w_q, w_v, w_kt, w_vt)


def _prompt_attn_kernel(bias_ref, q_ref, kt_ref, v_ref, u_ref, o_ref):
    hp = pl.program_id(1)
    qi = pl.program_id(2)
    tq = q_ref.shape[1]
    kb = ATT_K_BLOCK
    n_sub = tq // kb

    q = q_ref[0]
    first = lax.broadcasted_iota(jnp.int32, (tq, LANES), 1) < HEAD_DIM
    zero = jnp.zeros_like(q)
    q_heads = (jnp.where(first, q, zero), jnp.where(first, zero, q))
    biases = [bias_ref[hp * HEADS_PER_BLOCK + hh] for hh in range(HEADS_PER_BLOCK)]

    def stage1(base, diagonal):
        pending = []
        for s in reversed(range(n_sub)):
            kt = kt_ref[0, :, pl.ds(pl.multiple_of(base + s * kb, kb), kb)]
            r0 = s * kb if diagonal else 0
            if diagonal:
                row = lax.broadcasted_iota(jnp.int32, (tq - r0, kb), 0)
                col = lax.broadcasted_iota(jnp.int32, (tq - r0, kb), 1)
                visible = col < row
            for hh in range(HEADS_PER_BLOCK):
                z = _dot(q_heads[hh][r0:, :], kt) + biases[hh]
                sp = _softplus2(z)
                if diagonal:
                    sp = jnp.where(visible, sp, 0.0)
                csum = _dot(sp.astype(BF16), u_ref[...])
                d = z - csum
                if diagonal:
                    d = jnp.where(visible, d, MASKED_LOG2)
                pending.append((d, csum[:, 0:1]))
        return tuple(pending)

    def stage2(base, pending, state):
        state = list(state)
        for i, s in enumerate(reversed(range(n_sub))):
            v = v_ref[0, pl.ds(pl.multiple_of(base + s * kb, kb), kb), :]
            for hh in range(HEADS_PER_BLOCK):
                d, total = pending[i * HEADS_PER_BLOCK + hh]
                r0 = tq - d.shape[0]
                run, acc = state[hh]
                pv = _dot(jnp.exp2(d - run[r0:, :]).astype(BF16), v)
                if r0:
                    zeros = jnp.zeros((r0, 1), F32)
                    total = jnp.concatenate([zeros, total], axis=0)
                    pv = jnp.concatenate([jnp.broadcast_to(zeros, (r0, LANES)), pv], axis=0)
                state[hh] = (run + total, acc + pv)
        return tuple(state)

    state = tuple((jnp.zeros((tq, 1), F32), jnp.zeros((tq, LANES), F32)) for _ in range(HEADS_PER_BLOCK))
    state = stage2(qi * tq, stage1(qi * tq, True), state)

    odd = qi % 2

    def single(state):
        base = (qi - 1) * tq
        return stage2(base, stage1(base, False), state)

    state = lax.cond(odd == 1, single, lambda state: state, state)

    def body(it, state):
        base = (qi - 1 - odd - 2 * it) * tq
        p1 = stage1(base, False)
        p2 = stage1(base - tq, False)
        return stage2(base - tq, p2, stage2(base, p1, state))

    state = lax.fori_loop(0, qi // 2, body, state)
    o_ref[0] = jnp.where(first, state[0][1], state[1][1]).astype(o_ref.dtype)


def _prompt_attn(q, kt, v, bias, u):
    bsz, t_len, d = q.shape
    tq = ATT_Q_BLOCK
    return pl.pallas_call(
        _prompt_attn_kernel,
        grid=(bsz, d // LANES, t_len // tq),
        in_specs=[
            pl.BlockSpec(memory_space=pltpu.SMEM),
            pl.BlockSpec((1, tq, LANES), lambda b, h, i: (b, i, h)),
            pl.BlockSpec((1, LANES, t_len), lambda b, h, i: (b, h, 0)),
            pl.BlockSpec((1, t_len, LANES), lambda b, h, i: (b, 0, h)),
            pl.BlockSpec((ATT_K_BLOCK, ATT_K_BLOCK), lambda b, h, i: (0, 0)),
        ],
        out_specs=pl.BlockSpec((1, tq, LANES), lambda b, h, i: (b, i, h)),
        out_shape=jax.ShapeDtypeStruct((bsz, t_len, d), BF16),
        compiler_params=_params(("arbitrary", "arbitrary", "arbitrary")),
        name="prompt_attn",
    )(bias, q, kt, v, u)


def _sample_attn_kernel(pt_ref, q_ref, kn_ref, vn_ref, bias_ref, u_ref, *rest):
    n = PAGES_PER_STEP
    kt_pages, vt_pages = rest[:n], rest[n:2 * n]
    o_ref, qm_ref, run_ref, acc_ref = rest[2 * n:]
    g = pl.program_id(1)
    n_new = q_ref.shape[1]
    d = q_ref.shape[2]
    rows = n_new * N_HEADS
    step_keys = u_ref.shape[0]

    lane_head = lax.broadcasted_iota(jnp.int32, (N_HEADS, d), 1) // HEAD_DIM
    own_head = lane_head == lax.broadcasted_iota(jnp.int32, (N_HEADS, d), 0)

    def key_block(z, visible, pv):
        keys = z.shape[1]
        z = z + bias_ref[:, 0:keys]
        sp = _softplus2(z)
        if visible is not None:
            sp = jnp.where(visible, sp, 0.0)
        sums = _dot(sp.astype(BF16), u_ref[0:keys, :])
        run = run_ref[...]
        run_keys = run[:, 0:keys] if keys < LANES else jnp.concatenate([run] * (keys // LANES), axis=1)
        w = jnp.exp2(z - sums[:, 0:keys] - run_keys)
        if visible is not None:
            w = jnp.where(visible, w, 0.0)
        acc_ref[...] += pv(w.astype(BF16))
        run_ref[...] = run + sums[:, step_keys:]

    @pl.when(g == 0)
    def _():
        q = q_ref[0]
        qm_ref[...] = jnp.concatenate(
            [jnp.where(own_head, jnp.broadcast_to(q[t:t + 1, :], (N_HEADS, d)), 0.0) for t in range(n_new)],
            axis=0).astype(BF16)
        run_ref[...] = jnp.zeros_like(run_ref)
        acc_ref[...] = jnp.zeros_like(acc_ref)
        key = lax.broadcasted_iota(jnp.int32, (rows, NEW_ROWS), 1)
        qry = lax.broadcasted_iota(jnp.int32, (rows, NEW_ROWS), 0) // N_HEADS
        vn = vn_ref[0].astype(BF16)
        key_block(_dot_nt(qm_ref[...], kn_ref[0].astype(BF16)), key < qry, lambda w: _dot(w, vn))

    kt = jnp.concatenate([p[0].astype(BF16) for p in kt_pages], axis=1)
    vt = jnp.concatenate([p[0].astype(BF16) for p in vt_pages], axis=1)
    key_block(_dot(qm_ref[...], kt), None, lambda w: _dot_nt(w, vt))

    @pl.when(g == pl.num_programs(1) - 1)
    def _():
        for t in range(n_new):
            part = acc_ref[t * N_HEADS:(t + 1) * N_HEADS, :]
            o_ref[0, t:t + 1, :] = jnp.sum(jnp.where(own_head, part, 0.0), axis=0, keepdims=True)


def _sample_attn(q, k_new, v_new, bias_rows, u, cache_kt, cache_vt, page_table):
    bsz, n_new, d = q.shape
    n_pages = page_table.shape[1]
    n = PAGES_PER_STEP
    assert n_pages % n == 0
    rows = n_new * N_HEADS

    def page_map(s):
        return lambda b, g, pt: (pt[b * n_pages + (n_pages - (g + 1) * n + s)], 0, 0)

    page_specs = lambda: [pl.BlockSpec((1, d, PAGE_SIZE), page_map(s)) for s in range(n)]
    grid_spec = pltpu.PrefetchScalarGridSpec(
        num_scalar_prefetch=1,
        grid=(bsz, n_pages // n),
        in_specs=[
            pl.BlockSpec((1, n_new, d), lambda b, g, pt: (b, 0, 0)),
            pl.BlockSpec((1, NEW_ROWS, d), lambda b, g, pt: (b, 0, 0)),
            pl.BlockSpec((1, NEW_ROWS, d), lambda b, g, pt: (b, 0, 0)),
            pl.BlockSpec((rows, n * PAGE_SIZE), lambda b, g, pt: (0, 0)),
            _resident((n * PAGE_SIZE, n * PAGE_SIZE + LANES), lambda b, g, pt: (0, 0)),
        ] + page_specs() + page_specs(),
        out_specs=pl.BlockSpec((1, n_new, d), lambda b, g, pt: (b, 0, 0)),
        scratch_shapes=[
            pltpu.VMEM((rows, d), BF16),
            pltpu.VMEM((rows, LANES), F32),
            pltpu.VMEM((rows, d), F32),
        ],
    )
    return pl.pallas_call(
        _sample_attn_kernel,
        grid_spec=grid_spec,
        out_shape=jax.ShapeDtypeStruct((bsz, n_new, d), F32),
        compiler_params=_params(("arbitrary", "arbitrary")),
        name="sample_attn",
    )(page_table.reshape(-1), q, k_new, v_new, bias_rows, u, *([cache_kt] * n), *([cache_vt] * n))


def _suffix_ones(n):
    return (lax.broadcasted_iota(jnp.int32, (n, n), 0) >= lax.broadcasted_iota(jnp.int32, (n, n), 1)).astype(BF16)


def kernel(x_prompt, x_sample, state_conv, state_ffn, cache_k, cache_v, page_table, norm_mix, norm_ffn, norm_final, conv_w_in, conv_b_in, conv_w_dw, conv_b_dw, conv_ln_g, conv_ln_b, conv_w_out, conv_b_out, attn_w_qkv, attn_w_o, attn_b_logit, ffn_w_gate, ffn_w_up, ffn_w_dw, ffn_b_dw, ffn_w_down):
    d = D_MODEL
    bsz, seq, _ = x_prompt.shape
    dbs, dseq, _ = x_sample.shape
    row2 = lambda a: a.reshape(1, -1)

    w_in = conv_w_in[0].astype(BF16)
    w_out = conv_w_out[0].astype(BF16)
    w_qkv = attn_w_qkv[0].astype(BF16)
    w_o = attn_w_o[0].astype(BF16)
    w_gate = ffn_w_gate.astype(BF16)
    w_up = ffn_w_up.astype(BF16)
    w_down = ffn_w_down.astype(BF16)
    bias = attn_b_logit[0] * LOG2E
    g_final = row2(norm_final)

    def layer0(x, conv_init, ffn_init, *, shift, tm, cb, row_chunk):
        b, t, _ = x.shape
        out_proj = (row2(conv_ln_g[0]), row2(conv_ln_b[0]), w_out, row2(conv_b_out[0]))
        y, conv_st = _glu_conv(x, row2(norm_mix[0]), w_in, row2(conv_b_in[0]), conv_w_dw[0], row2(conv_b_dw[0]),
                               conv_init, out_proj, shift=shift, tm=tm, cb=cb, row_chunk=row_chunk)
        if cb == d:
            x = y
        else:
            x = _ln_out(y.reshape(b * t, d), x.reshape(b * t, d), *out_proj, tm=tm).reshape(b, t, d)
        x, ffn_st = _ffn(x, row2(norm_ffn[0]), w_gate[0], w_up[0], ffn_w_dw[0], row2(ffn_b_dw[0]), w_down[0],
                         ffn_init, g_final, None, shift=shift, tm=tm, final_norm=False)
        return x, conv_st, ffn_st

    def layer1_tail(x, att, ffn_init, *, shift, tm):
        return _ffn(x, row2(norm_ffn[1]), w_gate[1], w_up[1], ffn_w_dw[1], row2(ffn_b_dw[1]), w_down[1],
                    ffn_init, g_final, (att, w_o), shift=shift, tm=tm, final_norm=True)

    tm = 512
    zero_conv = jnp.zeros((1, CONV_WIDTH - 1, d), F32)
    zero_ffn = jnp.zeros((1, FFN_CONV_WIDTH - 1, D_FF), F32)
    xp, sc_p, sf0_p = layer0(x_prompt, zero_conv, zero_ffn, shift=1, tm=tm, cb=d, row_chunk=32)
    q, v, kt_p, vt_p, kt_bf = _qkv_t(xp, row2(norm_mix[1]), w_qkv[:, 0:d], w_qkv[:, 2 * d:3 * d],
                                     w_qkv[:, d:2 * d].T, w_qkv[:, 2 * d:3 * d].T, tm=tm)
    att = _prompt_attn(q, kt_bf, v, bias, _suffix_ones(ATT_K_BLOCK))
    y_p, sf1_p = layer1_tail(xp, att, zero_ffn, shift=1, tm=tm)

    n_tok = dbs * dseq
    to_tm = lambda a: a.transpose(1, 0, 2).reshape(1, a.shape[1] * dbs, a.shape[2])
    from_tm = lambda a: a.reshape(a.shape[1] // dbs, dbs, a.shape[2]).transpose(1, 0, 2)
    xs, sc_s, sf0_s = layer0(to_tm(x_sample), to_tm(state_conv[0]), to_tm(state_ffn[0]),
                             shift=dbs, tm=n_tok, cb=2 * LANES, row_chunk=64)
    q, k_s, v_s = _qkv(xs.reshape(n_tok, d), row2(norm_mix[1]), w_qkv, tm=n_tok)
    q, k_s, v_s = (from_tm(a.reshape(1, n_tok, d)) for a in (q, k_s, v_s))
    pad = ((0, 0), (0, NEW_ROWS - dseq), (0, 0))
    n_pool = cache_k.shape[1]
    pages_t = lambda c: c[0].transpose(0, 2, 3, 1).reshape(n_pool, d, PAGE_SIZE)
    step_keys = PAGES_PER_STEP * PAGE_SIZE
    bias_rows = jnp.broadcast_to(jnp.tile(bias, dseq)[:, None], (dseq * N_HEADS, step_keys))
    u_tot = jnp.concatenate([_suffix_ones(step_keys), jnp.ones((step_keys, LANES), BF16)], axis=1)
    att = _sample_attn(q, jnp.pad(k_s, pad), jnp.pad(v_s, pad), bias_rows, u_tot,
                       pages_t(cache_k), pages_t(cache_v), page_table)
    y_s, sf1_s = layer1_tail(xs, to_tm(att).astype(BF16), to_tm(state_ffn[1]), shift=dbs, tm=n_tok)

    heads_t = lambda a: a.reshape(bsz, N_HEADS, HEAD_DIM, seq).transpose(0, 3, 1, 2)[None]
    heads = lambda a: a.reshape(1, dbs, dseq, N_HEADS, HEAD_DIM)
    return (
        y_p,
        from_tm(y_s),
        sc_p[None],
        jnp.stack([sf0_p, sf1_p]),
        heads_t(kt_p),
        heads_t(vt_p),
        from_tm(sc_s)[None],
        jnp.stack([from_tm(sf0_s), from_tm(sf1_s)]),
        heads(k_s),
        heads(v_s),
    )
```

```python
import functools

import jax
import jax.numpy as jnp
from jax import lax
from jax.experimental import pallas as pl
from jax.experimental.pallas import tpu as pltpu

F32 = jnp.float32
BF16 = jnp.bfloat16

D_MODEL = 1024
D_FF = 2816
N_HEADS = 16
HEAD_DIM = 64
CONV_WIDTH = 31
FFN_CONV_WIDTH = 3
PAGE_SIZE = 128
EPS = 1e-6

SUBLANES = 8
LANES = 128
VMEM_BYTES_V7X = 64 * 1024 * 1024
VMEM_LIMIT = VMEM_BYTES_V7X - 8 * 1024 * 1024

HEADS_PER_BLOCK = LANES // HEAD_DIM
ATT_Q_BLOCK = 512
ATT_K_BLOCK = 256
PAGES_PER_STEP = 16
NEW_ROWS = 16
LOG2E = 1.4426950408889634
MASKED_LOG2 = -1e30


def _params(semantics):
    return pltpu.CompilerParams(dimension_semantics=semantics, vmem_limit_bytes=VMEM_LIMIT)


def _resident(shape, index_map):
    return pl.BlockSpec(shape, index_map, pipeline_mode=pl.Buffered(1))


def _rms(x, g):
    return x * lax.rsqrt(jnp.mean(x * x, axis=-1, keepdims=True) + EPS) * g


def _softplus2(z):
    return jnp.maximum(z, 0.0) + jnp.log(1.0 + jnp.exp2(-jnp.abs(z))) * LOG2E


def _silu(x):
    return x * jax.nn.sigmoid(x)


def _dot(a, b):
    return jnp.dot(a, b, preferred_element_type=F32)


def _dot_nt(a, b):
    return lax.dot_general(a, b, (((1,), (1,)), ((), ())), preferred_element_type=F32)


def _ln_silu(y, ln_g, ln_b):
    mu = jnp.mean(y, axis=-1, keepdims=True)
    yc = y - mu
    var = jnp.mean(yc * yc, axis=-1, keepdims=True)
    return _silu(yc * lax.rsqrt(var + EPS) * ln_g + ln_b)


def _glu_conv_kernel(x_ref, g_ref, wa_ref, wg_ref, ba_ref, bg_ref, wdw_ref, bdw_ref, init_ref, *rest,
                     shift, tm, n_tiles, row_chunk, fused):
    if fused:
        lg_ref, lb_ref, wo_ref, bo_ref, o_ref, st_ref, ext_ref, y_ref = rest
    else:
        y_ref, st_ref, ext_ref = rest
    t = pl.program_id(1)
    c = pl.program_id(2)
    past = (CONV_WIDTH - 1) * shift
    n_copies = ext_ref.shape[0]

    @pl.when(t == 0)
    def _():
        for r in range(n_copies):
            ext_ref[r, c, 0:past - r, :] = init_ref[0, r:past, :]

    h = _rms(x_ref[0], g_ref[...]).astype(BF16)
    a = _dot(h, wa_ref[...]) + ba_ref[...]
    gate = _dot(h, wg_ref[...]) + bg_ref[...]
    glu = a * jax.nn.sigmoid(gate)
    for r in range(n_copies):
        ext_ref[r, c, past - r:past - r + tm, :] = glu

    cb = y_ref.shape[-1]

    def conv_rows(i, carry):
        r0 = pl.multiple_of(i * row_chunk, row_chunk)
        groups = range(row_chunk // SUBLANES)
        accs = [jnp.broadcast_to(bdw_ref[...], (SUBLANES, cb)) for _ in groups]
        for j in range(CONV_WIDTH):
            r = (j * shift) % n_copies
            w = wdw_ref[j * SUBLANES:(j + 1) * SUBLANES, :]
            for gi in groups:
                rows = pl.ds(r0 + (j * shift - r + gi * SUBLANES), SUBLANES)
                accs[gi] = accs[gi] + w * ext_ref[r, c, rows, :]
        for gi in groups:
            y_ref[0, pl.ds(r0 + gi * SUBLANES, SUBLANES), :] = accs[gi]
        return carry

    lax.fori_loop(0, tm // row_chunk, conv_rows, 0)

    if fused:
        z = _ln_silu(y_ref[0], lg_ref[...], lb_ref[...])
        o_ref[0] = x_ref[0] + _dot(z.astype(BF16), wo_ref[...]) + bo_ref[...]

    @pl.when(t == n_tiles - 1)
    def _():
        st_ref[0] = ext_ref[0, c, tm:tm + past, :]

    if n_tiles > 1:
        for r in range(n_copies):
            ext_ref[r, c, 0:past - r, :] = ext_ref[r, c, tm:tm + past - r, :]


def _glu_conv(x, g, w_in, b_in, w_dw, b_dw, init, out_proj, *, shift, tm, cb, row_chunk):
    bsz, t_len, d = x.shape
    n_tiles = t_len // tm
    nc = d // cb
    past = (CONV_WIDTH - 1) * shift
    n_copies = 1 if shift % SUBLANES == 0 else SUBLANES
    assert shift == 1 or shift % SUBLANES == 0
    assert t_len % tm == 0 and d % cb == 0 and tm % row_chunk == 0 and row_chunk % SUBLANES == 0
    assert nc == 1 or n_tiles == 1
    assert n_tiles == 1 or past <= tm
    fused = nc == 1
    kern = functools.partial(_glu_conv_kernel, shift=shift, tm=tm, n_tiles=n_tiles, row_chunk=row_chunk,
                             fused=fused)
    vec = pl.BlockSpec((1, d), lambda b, t, c: (0, 0))
    fused_specs = [vec, vec, _resident((d, d), lambda b, t, c: (0, 0)), vec] if fused else []
    return pl.pallas_call(
        kern,
        grid=(bsz, n_tiles, nc),
        in_specs=[
            pl.BlockSpec((1, tm, d), lambda b, t, c: (b, t, 0)),
            pl.BlockSpec((1, d), lambda b, t, c: (0, 0)),
            pl.BlockSpec((d, cb), lambda b, t, c: (0, c)),
            pl.BlockSpec((d, cb), lambda b, t, c: (0, nc + c)),
            pl.BlockSpec((1, cb), lambda b, t, c: (0, c)),
            pl.BlockSpec((1, cb), lambda b, t, c: (0, nc + c)),
            pl.BlockSpec((CONV_WIDTH * SUBLANES, cb), lambda b, t, c: (0, c)),
            pl.BlockSpec((1, cb), lambda b, t, c: (0, c)),
            pl.BlockSpec((1, past, cb), lambda b, t, c: (0, 0, c)),
        ] + fused_specs,
        out_specs=[
            pl.BlockSpec((1, tm, cb), lambda b, t, c: (b, t, c)),
            pl.BlockSpec((1, past, cb), lambda b, t, c: (b, 0, c)),
        ],
        out_shape=[
            jax.ShapeDtypeStruct((bsz, t_len, d), F32),
            jax.ShapeDtypeStruct((bsz, past, d), F32),
        ],
        scratch_shapes=[pltpu.VMEM((n_copies, nc, past + tm, cb), F32)]
        + ([pltpu.VMEM((1, tm, cb), F32)] if fused else []),
        compiler_params=_params(("arbitrary", "arbitrary", "arbitrary")),
        name="glu_conv",
    )(x, g, w_in, w_in, b_in, b_in, jnp.repeat(w_dw, SUBLANES, axis=0), b_dw, init, *(out_proj if fused else ()))


def _ln_out_kernel(y_ref, x_ref, lg_ref, lb_ref, w_ref, b_ref, o_ref):
    z = _ln_silu(y_ref[...], lg_ref[...], lb_ref[...])
    o_ref[...] = x_ref[...] + _dot(z.astype(BF16), w_ref[...]) + b_ref[...]


def _ln_out(y, x, ln_g, ln_b, w_out, b_out, *, tm):
    m, d = x.shape
    return pl.pallas_call(
        _ln_out_kernel,
        grid=(m // tm,),
        in_specs=[
            pl.BlockSpec((tm, d), lambda i: (i, 0)),
            pl.BlockSpec((tm, d), lambda i: (i, 0)),
            pl.BlockSpec((1, d), lambda i: (0, 0)),
            pl.BlockSpec((1, d), lambda i: (0, 0)),
            _resident((d, d), lambda i: (0, 0)),
            pl.BlockSpec((1, d), lambda i: (0, 0)),
        ],
        out_specs=pl.BlockSpec((tm, d), lambda i: (i, 0)),
        out_shape=jax.ShapeDtypeStruct((m, d), F32),
        compiler_params=_params(("arbitrary",)),
        name="ln_out",
    )(y, x, ln_g, ln_b, w_out, b_out)


def _ffn_kernel(x_ref, g_ref, wg_ref, wu_ref, wdw_ref, bdw_ref, wd_ref, init_ref, gf_ref, *rest,
                shift, tm, n_tiles, fc, final_norm, with_attn):
    if with_attn:
        att_ref, wo_ref, o_ref, st_ref, ext_ref, h_ref, a_ref = rest
    else:
        o_ref, st_ref, ext_ref, h_ref, a_ref = rest
    t = pl.program_id(1)
    past = (FFN_CONV_WIDTH - 1) * shift
    off = (-past) % SUBLANES
    new = off + past

    @pl.when(t == 0)
    def _():
        ext_ref[off:new, :] = init_ref[0]

    x = x_ref[0]
    if with_attn:
        x = x + _dot(att_ref[0], wo_ref[...])
    h_ref[...] = _rms(x, g_ref[...]).astype(BF16)
    for k in range(D_FF // fc):
        cs = slice(k * fc, (k + 1) * fc)
        ext_ref[new:new + tm, cs] = _dot(h_ref[...], wg_ref[:, cs])
        gc = bdw_ref[:, cs]
        for j in range(FFN_CONV_WIDTH):
            gc = gc + wdw_ref[j:j + 1, cs] * ext_ref[off + j * shift:off + j * shift + tm, cs]
        up = _dot(h_ref[...], wu_ref[:, cs])
        a_ref[:, cs] = (_silu(gc) * up).astype(BF16)
    out = x + _dot(a_ref[...], wd_ref[...])
    if final_norm:
        out = _rms(out, gf_ref[...])
    o_ref[0] = out

    @pl.when(t == n_tiles - 1)
    def _():
        st_ref[0] = ext_ref[off + tm:off + tm + past, :]

    if n_tiles > 1:
        ext_ref[off:new, :] = ext_ref[off + tm:off + tm + past, :]


def _ffn(x, g, w_gate, w_up, w_dw, b_dw, w_down, init, g_final, attn, *, shift, tm, final_norm):
    bsz, t_len, d = x.shape
    f = w_gate.shape[1]
    n_tiles = t_len // tm
    past = (FFN_CONV_WIDTH - 1) * shift
    off = (-past) % SUBLANES
    assert t_len % tm == 0 and (n_tiles == 1 or past <= tm)
    kern = functools.partial(_ffn_kernel, shift=shift, tm=tm, n_tiles=n_tiles, fc=2 * LANES,
                             final_norm=final_norm, with_attn=attn is not None)
    attn_specs = [] if attn is None else [pl.BlockSpec((1, tm, d), lambda b, t: (b, t, 0)),
                                          _resident((d, d), lambda b, t: (0, 0))]
    return pl.pallas_call(
        kern,
        grid=(bsz, n_tiles),
        in_specs=[
            pl.BlockSpec((1, tm, d), lambda b, t: (b, t, 0)),
            pl.BlockSpec((1, d), lambda b, t: (0, 0)),
            _resident((d, f), lambda b, t: (0, 0)),
            _resident((d, f), lambda b, t: (0, 0)),
            pl.BlockSpec((FFN_CONV_WIDTH, f), lambda b, t: (0, 0)),
            pl.BlockSpec((1, f), lambda b, t: (0, 0)),
            _resident((f, d), lambda b, t: (0, 0)),
            pl.BlockSpec((1, past, f), lambda b, t: (0, 0, 0)),
            pl.BlockSpec((1, d), lambda b, t: (0, 0)),
        ] + attn_specs,
        out_specs=[
            pl.BlockSpec((1, tm, d), lambda b, t: (b, t, 0)),
            pl.BlockSpec((1, past, f), lambda b, t: (b, 0, 0)),
        ],
        out_shape=[
            jax.ShapeDtypeStruct((bsz, t_len, d), F32),
            jax.ShapeDtypeStruct((bsz, past, f), F32),
        ],
        scratch_shapes=[
            pltpu.VMEM((off + past + tm, f), F32),
            pltpu.VMEM((tm, d), BF16),
            pltpu.VMEM((tm, f), BF16),
        ],
        compiler_params=_params(("arbitrary", "arbitrary")),
        name="conv_ffn",
    )(x, g, w_gate, w_up, w_dw, b_dw, w_down, init, g_final, *(attn or ()))


Q_SCALE = LOG2E * HEAD_DIM ** -0.5


def _qkv_kernel(x_ref, g_ref, w_ref, q_ref, k_ref, v_ref):
    d = x_ref.shape[-1]
    h = _rms(x_ref[...], g_ref[...]).astype(BF16)
    q_ref[...] = _dot(h, w_ref[:, 0:d]) * Q_SCALE
    k_ref[...] = _dot(h, w_ref[:, d:2 * d])
    v_ref[...] = _dot(h, w_ref[:, 2 * d:3 * d])


def _qkv(x, g, w_qkv, *, tm):
    m, d = x.shape
    row = pl.BlockSpec((tm, d), lambda i: (i, 0))
    return pl.pallas_call(
        _qkv_kernel,
        grid=(m // tm,),
        in_specs=[row, pl.BlockSpec((1, d), lambda i: (0, 0)), _resident((d, 3 * d), lambda i: (0, 0))],
        out_specs=[row, row, row],
        out_shape=[jax.ShapeDtypeStruct((m, d), F32)] * 3,
        compiler_params=_params(("arbitrary",)),
        name="qkv_proj",
    )(x, g, w_qkv)


def _qkv_t_kernel(x_ref, g_ref, wq_ref, wv_ref, wkt_ref, wvt_ref, q_ref, v_ref, kt_ref, vt_ref, ktb_ref):
    h = _rms(x_ref[0], g_ref[...]).astype(BF16)
    q_ref[0] = (_dot(h, wq_ref[...]) * Q_SCALE).astype(BF16)
    v_ref[0] = _dot(h, wv_ref[...]).astype(BF16)
    kt = _dot_nt(wkt_ref[...], h)
    kt_ref[0] = kt
    ktb_ref[0] = kt.astype(BF16)
    vt_ref[0] = _dot_nt(wvt_ref[...], h)


def _qkv_t(x, g, w_q, w_v, w_kt, w_vt, *, tm):
    bsz, t_len, d = x.shape
    row = pl.BlockSpec((1, tm, d), lambda b, t: (b, t, 0))
    col = pl.BlockSpec((1, d, tm), lambda b, t: (b, 0, t))
    weight = lambda: _resident((d, d), lambda b, t: (0, 0))
    return pl.pallas_call(
        _qkv_t_kernel,
        grid=(bsz, t_len // tm),
        in_specs=[row, pl.BlockSpec((1, d), lambda b, t: (0, 0)), weight(), weight(), weight(), weight()],
        out_specs=[row, row, col, col, col],
        out_shape=[
            jax.ShapeDtypeStruct((bsz, t_len, d), BF16),
            jax.ShapeDtypeStruct((bsz, t_len, d), BF16),
            jax.ShapeDtypeStruct((bsz, d, t_len), F32),
            jax.ShapeDtypeStruct((bsz, d, t_len), F32),
            jax.ShapeDtypeStruct((bsz, d, t_len), BF16),
        ],
        compiler_params=_params(("arbitrary", "arbitrary")),
        name="qkv_proj_t",
    )(x, g, w_q, w_v, w_kt, w_vt)


def _prompt_attn_kernel(bias_ref, q_ref, kt_ref, v_ref, u_ref, o_ref):
    hp = pl.program_id(1)
    qi = pl.program_id(2)
    tq = q_ref.shape[1]
    kb = ATT_K_BLOCK
    n_sub = tq // kb

    q = q_ref[0]
    first = lax.broadcasted_iota(jnp.int32, (tq, LANES), 1) < HEAD_DIM
    zero = jnp.zeros_like(q)
    q_heads = (jnp.where(first, q, zero), jnp.where(first, zero, q))
    biases = [bias_ref[hp * HEADS_PER_BLOCK + hh] for hh in range(HEADS_PER_BLOCK)]

    def stage1(base, diagonal):
        pending = []
        for s in reversed(range(n_sub)):
            kt = kt_ref[0, :, pl.ds(pl.multiple_of(base + s * kb, kb), kb)]
            r0 = s * kb if diagonal else 0
            if diagonal:
                row = lax.broadcasted_iota(jnp.int32, (tq - r0, kb), 0)
                col = lax.broadcasted_iota(jnp.int32, (tq - r0, kb), 1)
                visible = col < row
            for hh in range(HEADS_PER_BLOCK):
                z = _dot(q_heads[hh][r0:, :], kt) + biases[hh]
                sp = _softplus2(z)
                if diagonal:
                    sp = jnp.where(visible, sp, 0.0)
                csum = _dot(sp.astype(BF16), u_ref[...])
                d = z - csum
                if diagonal:
                    d = jnp.where(visible, d, MASKED_LOG2)
                pending.append((d, csum[:, 0:1]))
        return tuple(pending)

    def stage2(base, pending, state):
        state = list(state)
        for i, s in enumerate(reversed(range(n_sub))):
            v = v_ref[0, pl.ds(pl.multiple_of(base + s * kb, kb), kb), :]
            for hh in range(HEADS_PER_BLOCK):
                d, total = pending[i * HEADS_PER_BLOCK + hh]
                r0 = tq - d.shape[0]
                run, acc = state[hh]
                pv = _dot(jnp.exp2(d - run[r0:, :]).astype(BF16), v)
                if r0:
                    zeros = jnp.zeros((r0, 1), F32)
                    total = jnp.concatenate([zeros, total], axis=0)
                    pv = jnp.concatenate([jnp.broadcast_to(zeros, (r0, LANES)), pv], axis=0)
                state[hh] = (run + total, acc + pv)
        return tuple(state)

    state = tuple((jnp.zeros((tq, 1), F32), jnp.zeros((tq, LANES), F32)) for _ in range(HEADS_PER_BLOCK))
    odd = qi % 2

    def diagonal_only(state):
        return stage2(qi * tq, stage1(qi * tq, True), state)

    def diagonal_and_next(state):
        diag = stage1(qi * tq, True)
        older = stage1((qi - 1) * tq, False)
        return stage2((qi - 1) * tq, older, stage2(qi * tq, diag, state))

    state = lax.cond(odd == 1, diagonal_and_next, diagonal_only, state)

    def body(it, state):
        base = (qi - 1 - odd - 2 * it) * tq
        p1 = stage1(base, False)
        p2 = stage1(base - tq, False)
        return stage2(base - tq, p2, stage2(base, p1, state))

    state = lax.fori_loop(0, qi // 2, body, state)
    o_ref[0] = jnp.where(first, state[0][1], state[1][1]).astype(o_ref.dtype)


def _prompt_attn(q, kt, v, bias, u):
    bsz, t_len, d = q.shape
    tq = ATT_Q_BLOCK
    return pl.pallas_call(
        _prompt_attn_kernel,
        grid=(bsz, d // LANES, t_len // tq),
        in_specs=[
            pl.BlockSpec(memory_space=pltpu.SMEM),
            pl.BlockSpec((1, tq, LANES), lambda b, h, i: (b, i, h)),
            pl.BlockSpec((1, LANES, t_len), lambda b, h, i: (b, h, 0)),
            pl.BlockSpec((1, t_len, LANES), lambda b, h, i: (b, 0, h)),
            pl.BlockSpec((ATT_K_BLOCK, ATT_K_BLOCK), lambda b, h, i: (0, 0)),
        ],
        out_specs=pl.BlockSpec((1, tq, LANES), lambda b, h, i: (b, i, h)),
        out_shape=jax.ShapeDtypeStruct((bsz, t_len, d), BF16),
        compiler_params=_params(("arbitrary", "arbitrary", "arbitrary")),
        name="prompt_attn",
    )(bias, q, kt, v, u)


def _sample_attn_kernel(pt_ref, q_ref, kn_ref, vn_ref, bias_ref, u_ref, *rest):
    n = PAGES_PER_STEP
    kt_pages, vt_pages = rest[:n], rest[n:2 * n]
    o_ref, qm_ref, run_ref, acc_ref = rest[2 * n:]
    g = pl.program_id(1)
    n_new = q_ref.shape[1]
    d = q_ref.shape[2]
    rows = n_new * N_HEADS
    step_keys = u_ref.shape[0]

    lane_head = lax.broadcasted_iota(jnp.int32, (N_HEADS, d), 1) // HEAD_DIM
    own_head = lane_head == lax.broadcasted_iota(jnp.int32, (N_HEADS, d), 0)

    def key_block(z, visible, pv):
        keys = z.shape[1]
        z = z + bias_ref[:, 0:keys]
        sp = _softplus2(z)
        if visible is not None:
            sp = jnp.where(visible, sp, 0.0)
        sums = _dot(sp.astype(BF16), u_ref[0:keys, :])
        run = run_ref[...]
        run_keys = run[:, 0:keys] if keys < LANES else jnp.concatenate([run] * (keys // LANES), axis=1)
        w = jnp.exp2(z - sums[:, 0:keys] - run_keys)
        if visible is not None:
            w = jnp.where(visible, w, 0.0)
        acc_ref[...] += pv(w.astype(BF16))
        run_ref[...] = run + sums[:, step_keys:]

    @pl.when(g == 0)
    def _():
        q = q_ref[0]
        qm_ref[...] = jnp.concatenate(
            [jnp.where(own_head, jnp.broadcast_to(q[t:t + 1, :], (N_HEADS, d)), 0.0) for t in range(n_new)],
            axis=0).astype(BF16)
        run_ref[...] = jnp.zeros_like(run_ref)
        acc_ref[...] = jnp.zeros_like(acc_ref)
        key = lax.broadcasted_iota(jnp.int32, (rows, NEW_ROWS), 1)
        qry = lax.broadcasted_iota(jnp.int32, (rows, NEW_ROWS), 0) // N_HEADS
        vn = vn_ref[0].astype(BF16)
        key_block(_dot_nt(qm_ref[...], kn_ref[0].astype(BF16)), key < qry, lambda w: _dot(w, vn))

    kt = jnp.concatenate([p[0].astype(BF16) for p in kt_pages], axis=1)
    vt = jnp.concatenate([p[0].astype(BF16) for p in vt_pages], axis=1)
    key_block(_dot(qm_ref[...], kt), None, lambda w: _dot_nt(w, vt))

    @pl.when(g == pl.num_programs(1) - 1)
    def _():
        for t in range(n_new):
            part = acc_ref[t * N_HEADS:(t + 1) * N_HEADS, :]
            o_ref[0, t:t + 1, :] = jnp.sum(jnp.where(own_head, part, 0.0), axis=0, keepdims=True)


def _sample_attn(q, k_new, v_new, bias_rows, u, cache_kt, cache_vt, page_table):
    bsz, n_new, d = q.shape
    n_pages = page_table.shape[1]
    n = PAGES_PER_STEP
    assert n_pages % n == 0
    rows = n_new * N_HEADS

    def page_map(s):
        return lambda b, g, pt: (pt[b * n_pages + (n_pages - (g + 1) * n + s)], 0, 0)

    page_specs = lambda: [pl.BlockSpec((1, d, PAGE_SIZE), page_map(s)) for s in range(n)]
    grid_spec = pltpu.PrefetchScalarGridSpec(
        num_scalar_prefetch=1,
        grid=(bsz, n_pages // n),
        in_specs=[
            pl.BlockSpec((1, n_new, d), lambda b, g, pt: (b, 0, 0)),
            pl.BlockSpec((1, NEW_ROWS, d), lambda b, g, pt: (b, 0, 0)),
            pl.BlockSpec((1, NEW_ROWS, d), lambda b, g, pt: (b, 0, 0)),
            pl.BlockSpec((rows, n * PAGE_SIZE), lambda b, g, pt: (0, 0)),
            _resident((n * PAGE_SIZE, n * PAGE_SIZE + LANES), lambda b, g, pt: (0, 0)),
        ] + page_specs() + page_specs(),
        out_specs=pl.BlockSpec((1, n_new, d), lambda b, g, pt: (b, 0, 0)),
        scratch_shapes=[
            pltpu.VMEM((rows, d), BF16),
            pltpu.VMEM((rows, LANES), F32),
            pltpu.VMEM((rows, d), F32),
        ],
    )
    return pl.pallas_call(
        _sample_attn_kernel,
        grid_spec=grid_spec,
        out_shape=jax.ShapeDtypeStruct((bsz, n_new, d), F32),
        compiler_params=_params(("arbitrary", "arbitrary")),
        name="sample_attn",
    )(page_table.reshape(-1), q, k_new, v_new, bias_rows, u, *([cache_kt] * n), *([cache_vt] * n))


def _suffix_ones(n):
    return (lax.broadcasted_iota(jnp.int32, (n, n), 0) >= lax.broadcasted_iota(jnp.int32, (n, n), 1)).astype(BF16)


def kernel(x_prompt, x_sample, state_conv, state_ffn, cache_k, cache_v, page_table, norm_mix, norm_ffn, norm_final, conv_w_in, conv_b_in, conv_w_dw, conv_b_dw, conv_ln_g, conv_ln_b, conv_w_out, conv_b_out, attn_w_qkv, attn_w_o, attn_b_logit, ffn_w_gate, ffn_w_up, ffn_w_dw, ffn_b_dw, ffn_w_down):
    d = D_MODEL
    bsz, seq, _ = x_prompt.shape
    dbs, dseq, _ = x_sample.shape
    row2 = lambda a: a.reshape(1, -1)

    w_in = conv_w_in[0].astype(BF16)
    w_out = conv_w_out[0].astype(BF16)
    w_qkv = attn_w_qkv[0].astype(BF16)
    w_o = attn_w_o[0].astype(BF16)
    w_gate = ffn_w_gate.astype(BF16)
    w_up = ffn_w_up.astype(BF16)
    w_down = ffn_w_down.astype(BF16)
    bias = attn_b_logit[0] * LOG2E
    g_final = row2(norm_final)

    def layer0(x, conv_init, ffn_init, *, shift, tm, cb, row_chunk):
        b, t, _ = x.shape
        out_proj = (row2(conv_ln_g[0]), row2(conv_ln_b[0]), w_out, row2(conv_b_out[0]))
        y, conv_st = _glu_conv(x, row2(norm_mix[0]), w_in, row2(conv_b_in[0]), conv_w_dw[0], row2(conv_b_dw[0]),
                               conv_init, out_proj, shift=shift, tm=tm, cb=cb, row_chunk=row_chunk)
        if cb == d:
            x = y
        else:
            x = _ln_out(y.reshape(b * t, d), x.reshape(b * t, d), *out_proj, tm=tm).reshape(b, t, d)
        x, ffn_st = _ffn(x, row2(norm_ffn[0]), w_gate[0], w_up[0], ffn_w_dw[0], row2(ffn_b_dw[0]), w_down[0],
                         ffn_init, g_final, None, shift=shift, tm=tm, final_norm=False)
        return x, conv_st, ffn_st

    def layer1_tail(x, att, ffn_init, *, shift, tm):
        return _ffn(x, row2(norm_ffn[1]), w_gate[1], w_up[1], ffn_w_dw[1], row2(ffn_b_dw[1]), w_down[1],
                    ffn_init, g_final, (att, w_o), shift=shift, tm=tm, final_norm=True)

    tm = 512
    zero_conv = jnp.zeros((1, CONV_WIDTH - 1, d), F32)
    zero_ffn = jnp.zeros((1, FFN_CONV_WIDTH - 1, D_FF), F32)
    xp, sc_p, sf0_p = layer0(x_prompt, zero_conv, zero_ffn, shift=1, tm=tm, cb=d, row_chunk=32)
    q, v, kt_p, vt_p, kt_bf = _qkv_t(xp, row2(norm_mix[1]), w_qkv[:, 0:d], w_qkv[:, 2 * d:3 * d],
                                     w_qkv[:, d:2 * d].T, w_qkv[:, 2 * d:3 * d].T, tm=tm)
    att = _prompt_attn(q, kt_bf, v, bias, _suffix_ones(ATT_K_BLOCK))
    y_p, sf1_p = layer1_tail(xp, att, zero_ffn, shift=1, tm=tm)

    n_tok = dbs * dseq
    to_tm = lambda a: a.transpose(1, 0, 2).reshape(1, a.shape[1] * dbs, a.shape[2])
    from_tm = lambda a: a.reshape(a.shape[1] // dbs, dbs, a.shape[2]).transpose(1, 0, 2)
    xs, sc_s, sf0_s = layer0(to_tm(x_sample), to_tm(state_conv[0]), to_tm(state_ffn[0]),
                             shift=dbs, tm=n_tok, cb=2 * LANES, row_chunk=64)
    q, k_s, v_s = _qkv(xs.reshape(n_tok, d), row2(norm_mix[1]), w_qkv, tm=n_tok)
    q, k_s, v_s = (from_tm(a.reshape(1, n_tok, d)) for a in (q, k_s, v_s))
    pad = ((0, 0), (0, NEW_ROWS - dseq), (0, 0))
    n_pool = cache_k.shape[1]
    pages_t = lambda c: c[0].transpose(0, 2, 3, 1).reshape(n_pool, d, PAGE_SIZE)
    step_keys = PAGES_PER_STEP * PAGE_SIZE
    bias_rows = jnp.broadcast_to(jnp.tile(bias, dseq)[:, None], (dseq * N_HEADS, step_keys))
    u_tot = jnp.concatenate([_suffix_ones(step_keys), jnp.ones((step_keys, LANES), BF16)], axis=1)
    att = _sample_attn(q, jnp.pad(k_s, pad), jnp.pad(v_s, pad), bias_rows, u_tot,
                       pages_t(cache_k), pages_t(cache_v), page_table)
    y_s, sf1_s = layer1_tail(xs, to_tm(att).astype(BF16), to_tm(state_ffn[1]), shift=dbs, tm=n_tok)

    heads_t = lambda a: a.reshape(bsz, N_HEADS, HEAD_DIM, seq).transpose(0, 3, 1, 2)[None]
    heads = lambda a: a.reshape(1, dbs, dseq, N_HEADS, HEAD_DIM)
    return (
        y_p,
        from_tm(y_s),
        sc_p[None],
        jnp.stack([sf0_p, sf1_p]),
        heads_t(kt_p),
        heads_t(vt_p),
        from_tm(sc_s)[None],
        jnp.stack([from_tm(sf0_s), from_tm(sf1_s)]),
        heads(k_s),
        heads(v_s),
    )
```
